```python
import jax, jax.numpy as jnp
from jax import lax
import numpy as np

D_MODEL = 1024
BATCH = 16
SEQ = 2048
DEPTH = 4

N_BRANCH = 4
MIX_W = D_MODEL // 2
RW_N = 64
RW_H = MIX_W // RW_N
RW_RANK_W = 64
RW_RANK_A = 64
RW_RANK_G = 128
RW_GN_EPS = 64e-5
RW_COLS = 3 * MIX_W + RW_RANK_W + RW_RANK_A + RW_RANK_G
ML_H = 4
ML_DV = MIX_W // ML_H
ML_DK = ML_DV // 2
ML_CHUNK = 64
ML_COLS = 2 * ML_H * ML_DK + 2 * MIX_W + 4 * ML_H
LRU_BLOCKS = 8
LRU_BW = MIX_W // LRU_BLOCKS
LRU_CONV = 4
RG_C = 8.0
LRU_COLS = 2 * MIX_W
HG_D = 128
HG_H = MIX_W // HG_D
HG_CHUNK = 64
HG_COLS = 5 * MIX_W
D_FF = 2816
FFN_CONV = 3

NEG_BIG = -1e30

IN_SPLITS = (RW_COLS, ML_COLS, LRU_COLS, HG_COLS, N_BRANCH * D_MODEL)
N_IN = sum(IN_SPLITS)

kernel_name = "hybrid_rwkv7_mlstm_rglru_hgrn2_encoder"


def _split(t, sizes):
    idx = [int(s) for s in np.cumsum(sizes)[:-1]]
    return jnp.split(t, idx, axis=-1)


def _rmsnorm(x, g, eps=1e-6):
    xf = x.astype(jnp.float32)
    return xf * lax.rsqrt(jnp.mean(xf * xf, axis=-1, keepdims=True) + eps) * g


def _head_rmsnorm(y, g, eps=1e-6):
    yf = y.astype(jnp.float32)
    yf = yf * lax.rsqrt(jnp.mean(yf * yf, axis=-1, keepdims=True) + eps)
    return yf.reshape(*y.shape[:-2], -1) * g


def _dwconv(x, w, b):
    k = w.shape[0]
    left = (k - 1) // 2
    s = x.shape[1]
    xp = jnp.pad(x, ((0, 0), (left, k - 1 - left), (0, 0)))
    out = xp[:, 0:s] * w[0]
    for j in range(1, k):
        out = out + xp[:, j:j + s] * w[j]
    return out + b


def _rwkv7_scan(r, logw, a, b, k, v):
    bsz, _, nh, n = r.shape

    def step(state, inp):
        r_t, lw_t, a_t, b_t, k_t, v_t = inp
        sa = jnp.einsum("bhvk,bhk->bhv", state, a_t)
        state = (state * jnp.exp(lw_t)[:, :, None, :]
                 + sa[..., None] * b_t[:, :, None, :]
                 + v_t[..., None] * k_t[:, :, None, :])
        return state, jnp.einsum("bhvk,bhk->bhv", state, r_t)

    xs = tuple(jnp.moveaxis(t, 1, 0) for t in (r, logw, a, b, k, v))
    _, y = lax.scan(step, jnp.zeros((bsz, nh, n, n), jnp.float32), xs)
    return jnp.moveaxis(y, 0, 1)


def _rwkv7_branch(p, mu, w0, w2, a0, a2, g2, k_k, k_a, r_k, ln_w, ln_b):
    bsz, s, _ = p.shape
    pf = p.astype(jnp.float32)
    prev = jnp.pad(pf, ((0, 0), (1, 0), (0, 0)))[:, :-1]
    nxt = jnp.pad(pf, ((0, 0), (0, 1), (0, 0)))[:, 1:]
    pf = pf + mu * (0.5 * (prev + nxt) - pf)
    r, k, v, xw, xa, xg = _split(pf, [MIX_W, MIX_W, MIX_W, RW_RANK_W, RW_RANK_A, RW_RANK_G])
    heads = lambda t: t.reshape(bsz, s, RW_H, RW_N)
    kk = heads(k * k_k)
    kk = kk * lax.rsqrt(jnp.maximum(jnp.sum(kk * kk, axis=-1, keepdims=True), 1e-24))
    r_h, v_h = heads(r), heads(v)
    ys, kts = [], []
    for d in range(2):
        w = -jax.nn.softplus(-(w0[d] + jnp.tanh(xw) @ w2[d])) - 0.5
        a = jax.nn.sigmoid(a0[d] + xa @ a2[d])
        kt = heads(k * (1.0 + (a - 1.0) * k_a))
        args = [r_h, heads(-jnp.exp(w)), -kk, kk * heads(a), kt, v_h]
        if d == 1:
            args = [jnp.flip(t, 1) for t in args]
        yd = _rwkv7_scan(*args)
        ys.append(yd if d == 0 else jnp.flip(yd, 1))
        kts.append(kt)
    y = ys[0] + ys[1]
    mean = jnp.mean(y, axis=-1, keepdims=True)
    var = jnp.mean(jnp.square(y - mean), axis=-1, keepdims=True)
    y = ((y - mean) * lax.rsqrt(var + RW_GN_EPS)).reshape(bsz, s, MIX_W) * ln_w + ln_b
    k_bonus = 0.5 * (kts[0] + kts[1])
    bonus = jnp.sum(r_h * k_bonus * r_k, axis=-1, keepdims=True) * v_h
    y = y + bonus.reshape(bsz, s, MIX_W)
    g = jax.nn.sigmoid(xg) @ g2
    return y * g


def _mlstm_chunkwise(q, k, v, i_pre, f_pre):
    bsz, nh, s, dk = q.shape
    dv = v.shape[-1]
    lc = ML_CHUNK
    nc = s // lc
    q = (q * dk ** -0.5).reshape(bsz, nh, nc, lc, dk)
    k = k.reshape(bsz, nh, nc, lc, dk)
    v = v.reshape(bsz, nh, nc, lc, dv)
    ig = i_pre.reshape(bsz, nh, nc, lc)
    bcum = jnp.cumsum(jax.nn.log_sigmoid(f_pre).reshape(bsz, nh, nc, lc), axis=-1)
    g_end = bcum[..., -1]
    a_end = g_end[..., None] - bcum + ig

    def step(carry, inp):
        c_st, n_st, m_st = carry
        k_c, v_c, a_c, g_c = inp
        m_new = jnp.maximum(g_c + m_st, jnp.max(a_c, axis=-1))
        w_c = jnp.exp(a_c - m_new[..., None])
        dec = jnp.exp(g_c + m_st - m_new)
        c_new = dec[..., None, None] * c_st + jnp.einsum("bhl,bhlk,bhlv->bhkv", w_c, k_c, v_c)
        n_new = dec[..., None] * n_st + jnp.einsum("bhl,bhlk->bhk", w_c, k_c)
        return (c_new, n_new, m_new), (c_st, n_st, m_st)

    init = (jnp.zeros((bsz, nh, dk, dv), jnp.float32),
            jnp.zeros((bsz, nh, dk), jnp.float32),
            jnp.zeros((bsz, nh), jnp.float32))
    xs = (jnp.moveaxis(k, 2, 0), jnp.moveaxis(v, 2, 0),
          jnp.moveaxis(a_end, 2, 0), jnp.moveaxis(g_end, 2, 0))
    _, (c_prev, n_prev, m_prev) = lax.scan(step, init, xs)
    c_prev = jnp.moveaxis(c_prev, 0, 2)
    n_prev = jnp.moveaxis(n_prev, 0, 2)
    m_prev = jnp.moveaxis(m_prev, 0, 2)
    lower = jnp.tril(jnp.ones((lc, lc), dtype=bool))
    log_d = jnp.where(lower, bcum[..., :, None] - bcum[..., None, :] + ig[..., None, :], NEG_BIG)
    m_inter = bcum + m_prev[..., None]
    m_t = jnp.maximum(m_inter, jnp.max(log_d, axis=-1))
    pw = (jnp.where(lower, jnp.exp(log_d - m_t[..., None]), 0.0)
          * jnp.einsum("bhcld,bhcsd->bhcls", q, k))
    s_inter = jnp.exp(m_inter - m_t)
    num = (s_inter[..., None] * jnp.einsum("bhcld,bhcdv->bhclv", q, c_prev)
           + jnp.einsum("bhcls,bhcsv->bhclv", pw, v))
    den = s_inter * jnp.einsum("bhcld,bhcd->bhcl", q, n_prev) + jnp.sum(pw, axis=-1)
    h = num / jnp.maximum(jnp.abs(den), jnp.exp(-m_t))[..., None]
    return h.reshape(bsz, nh, s, dv)


def _mlstm_branch(p, i_bias, f_bias, norm_g):
    bsz, s, _ = p.shape
    q, k, v, o, ig, fg = _split(p.astype(jnp.float32),
                                [ML_H * ML_DK, ML_H * ML_DK, MIX_W, MIX_W, 2 * ML_H, 2 * ML_H])
    heads = lambda t, d: t.reshape(bsz, s, ML_H, d).transpose(0, 2, 1, 3)
    q, k, v = heads(q, ML_DK), heads(k, ML_DK), heads(v, ML_DV)
    ig = (ig.reshape(bsz, s, 2, ML_H) + i_bias).transpose(2, 0, 3, 1)
    fg = (fg.reshape(bsz, s, 2, ML_H) + f_bias).transpose(2, 0, 3, 1)
    fl = lambda t: jnp.flip(t, 2)
    h = (_mlstm_chunkwise(q, k, v, ig[0], fg[0])
         + fl(_mlstm_chunkwise(fl(q), fl(k), fl(v), fl(ig[1]), fl(fg[1]))))
    h = h.transpose(0, 2, 1, 3)
    return jax.nn.sigmoid(o) * _head_rmsnorm(h, norm_g)


def _lin_comb(e1, e2):
    a1, b1 = e1
    a2, b2 = e2
    return a1 * a2, a2 * b1 + b2


def _rglru_branch(p, conv_w, conv_b, wa, ba, wx, bx, lam):
    bsz, s, _ = p.shape
    xb, gb = _split(p.astype(jnp.float32), [MIX_W, MIX_W])
    xc = _dwconv(xb, conv_w, conv_b)
    xblk = xc.reshape(bsz, s, LRU_BLOCKS, LRU_BW)
    y = None
    for d in range(2):
        ra = jnp.einsum("bsgi,gij->bsgj", xblk, wa[d]).reshape(bsz, s, MIX_W) + ba[d]
        rx = jnp.einsum("bsgi,gij->bsgj", xblk, wx[d]).reshape(bsz, s, MIX_W) + bx[d]
        log_a = -RG_C * jax.nn.softplus(-lam[d]) * jax.nn.sigmoid(ra)
        a = jnp.exp(log_a)
        mult = jnp.sqrt(jnp.maximum(-jnp.expm1(2.0 * log_a), 0.0))
        bin_ = mult * (jax.nn.sigmoid(rx) * xc)
        _, hd = lax.associative_scan(_lin_comb, (a, bin_), axis=1, reverse=(d == 1))
        y = hd if y is None else y + hd
    return y * jax.nn.gelu(gb)


def _hgrn2_chunkwise(q, k, v, logf):
    bsz, nh, s, dk = q.shape
    dv = v.shape[-1]
    lc = HG_CHUNK
    nc = s // lc
    chunks = lambda t: jnp.moveaxis(t.reshape(bsz, nh, nc, lc, t.shape[-1]), 2, 0)
    lower = jnp.tril(jnp.ones((lc, lc), dtype=bool))[:, :, None]

    def step(state, inp):
        q_c, k_c, v_c, lf_c = inp
        gc = jnp.cumsum(lf_c, axis=-2)
        diff = gc[:, :, :, None, :] - gc[:, :, None, :, :]
        dec = jnp.where(lower, jnp.exp(jnp.minimum(diff, 0.0)), 0.0)
        att = jnp.einsum("bhtd,bhsd,bhtsd->bhts", q_c, k_c, dec)
        o = (jnp.einsum("bhts,bhsv->bhtv", att, v_c)
             + jnp.einsum("bhtd,bhdv->bhtv", q_c * jnp.exp(gc), state))
        g_end = gc[:, :, -1:, :]
        state = (jnp.exp(g_end[:, :, 0, :, None]) * state
                 + jnp.einsum("bhsd,bhsv->bhdv", k_c * jnp.exp(g_end - gc), v_c))
        return state, o

    _, o = lax.scan(step, jnp.zeros((bsz, nh, dk, dv), jnp.float32),
                    (chunks(q), chunks(k), chunks(v), chunks(logf)))
    return jnp.moveaxis(o, 0, 2).reshape(bsz, nh, s, dv)


def _hgrn2_branch(p, lb, norm_g):
    bsz, s, _ = p.shape
    q, f_fw, f_bw, i, g = _split(p.astype(jnp.float32), [MIX_W] * 5)
    heads = lambda t: t.reshape(bsz, s, HG_H, HG_D).transpose(0, 2, 1, 3)
    qh, ih = heads(jax.nn.silu(q)), heads(i)
    fl = lambda t: jnp.flip(t, 2)
    outs = []
    for d, fp in enumerate((f_fw, f_bw)):
        f = lb[d] + (1.0 - lb[d]) * jax.nn.sigmoid(fp)
        logf = jnp.log(f)
        kd = (1.0 - lb[d]) * jax.nn.sigmoid(-fp)
        args = [qh, heads(kd), ih, heads(logf)]
        if d == 1:
            args = [fl(t) for t in args]
        od = _hgrn2_chunkwise(*args)
        outs.append(od if d == 0 else fl(od))
    o = (outs[0] + outs[1]).transpose(0, 2, 1, 3)
    return jax.nn.silu(g) * _head_rmsnorm(o, norm_g)


def _conv_ffn(u, w_up, cw, cb, w_down):
    z = _dwconv(u @ w_up, cw, cb)
    val, gate = jnp.split(z, 2, axis=-1)
    return (val * jax.nn.silu(gate)) @ w_down


def setup_inputs(seed: int = 0) -> dict:
    key = jax.random.key(seed)
    ks = iter(jax.random.split(key, 48))
    f32 = jnp.float32

    def nrm(shape, std=1.0):
        return std * jax.random.normal(next(ks), shape, f32)

    def unif(shape, lo, hi):
        return jax.random.uniform(next(ks), shape, f32, lo, hi)

    L, D, W = DEPTH, D_MODEL, MIX_W
    lam_s = unif((L, 2, W), 0.9, 0.999) ** (1.0 / RG_C)
    return {
        "x": nrm((BATCH, SEQ, D)),
        "c": nrm((BATCH, D)),
        "ada_w": nrm((L, D, 6 * D), D ** -0.5),
        "ada_b": nrm((L, 6 * D), 0.02),
        "norm1_g": 1.0 + nrm((L, D), 0.02),
        "w_in": nrm((L, D, N_IN), D ** -0.5),
        "rw_mu": unif((L, RW_COLS), 0.0, 1.0),
        "rw_w0": unif((L, 2, W), -5.0, 0.0),
        "rw_w2": nrm((L, 2, RW_RANK_W, W), 0.1 * RW_RANK_W ** -0.5),
        "rw_a0": nrm((L, 2, W), 0.1),
        "rw_a2": nrm((L, 2, RW_RANK_A, W), 0.1 * RW_RANK_A ** -0.5),
        "rw_g2": nrm((L, RW_RANK_G, W), RW_RANK_G ** -0.5),
        "rw_kk": 0.85 + nrm((L, W), 0.05),
        "rw_ka": 1.0 + nrm((L, W), 0.05),
        "rw_rk": nrm((L, RW_H, RW_N), 0.1),
        "rw_lnw": 1.0 + nrm((L, W), 0.02),
        "rw_lnb": nrm((L, W), 0.02),
        "ml_ibias": nrm((L, 2, ML_H), 0.5),
        "ml_fbias": unif((L, 2, ML_H), 3.0, 6.0),
        "ml_norm": 1.0 + nrm((L, W), 0.02),
        "lru_conv_w": nrm((L, LRU_CONV, W), LRU_CONV ** -0.5),
        "lru_conv_b": nrm((L, W), 0.02),
        "lru_wa": nrm((L, 2, LRU_BLOCKS, LRU_BW, LRU_BW), LRU_BW ** -0.5),
        "lru_ba": nrm((L, 2, W), 0.02),
        "lru_wx": nrm((L, 2, LRU_BLOCKS, LRU_BW, LRU_BW), LRU_BW ** -0.5),
        "lru_bx": nrm((L, 2, W), 0.02),
        "lru_lam": jnp.log(lam_s) - jnp.log1p(-lam_s),
        "hg_lb": nrm((L, 2, W)),
        "hg_norm": 1.0 + nrm((L, W), 0.02),
        "w_branch": nrm((L, N_BRANCH, W, D), W ** -0.5),
        "w_out": nrm((L, D, D), D ** -0.5),
        "norm2_g": 1.0 + nrm((L, D), 0.02),
        "ffn_up": nrm((L, D, 2 * D_FF), D ** -0.5),
        "ffn_conv_w": nrm((L, FFN_CONV, 2 * D_FF), FFN_CONV ** -0.5),
        "ffn_conv_b": nrm((L, 2 * D_FF), 0.02),
        "ffn_down": nrm((L, D_FF, D), D_FF ** -0.5),
        "final_g": 1.0 + nrm((D,), 0.02),
    }


def reference(x, c, ada_w, ada_b, norm1_g, w_in, rw_mu, rw_w0, rw_w2, rw_a0, rw_a2, rw_g2,
              rw_kk, rw_ka, rw_rk, rw_lnw, rw_lnb, ml_ibias, ml_fbias, ml_norm,
              lru_conv_w, lru_conv_b, lru_wa, lru_ba, lru_wx, lru_bx, lru_lam,
              hg_lb, hg_norm, w_branch, w_out, norm2_g, ffn_up, ffn_conv_w, ffn_conv_b,
              ffn_down, final_g):
    lb_soft = jax.nn.softmax(hg_lb.astype(jnp.float32), axis=0)
    lb_all = jnp.clip(jnp.cumsum(lb_soft, axis=0) - lb_soft[0], 0.0, 1.0)
    cond = jax.nn.silu(c.astype(jnp.float32))
    h = x.astype(jnp.float32)
    for l in range(DEPTH):
        mod = cond @ ada_w[l] + ada_b[l]
        sh1, sc1, gt1, sh2, sc2, gt2 = jnp.split(mod[:, None, :], 6, axis=-1)
        u = _rmsnorm(h, norm1_g[l]) * (1.0 + sc1) + sh1
        p_rw, p_ml, p_lru, p_hg, p_gate = _split(u @ w_in[l], IN_SPLITS)
        ys = (
            _rwkv7_branch(p_rw, rw_mu[l], rw_w0[l], rw_w2[l], rw_a0[l], rw_a2[l], rw_g2[l],
                          rw_kk[l], rw_ka[l], rw_rk[l], rw_lnw[l], rw_lnb[l]),
            _mlstm_branch(p_ml, ml_ibias[l], ml_fbias[l], ml_norm[l]),
            _rglru_branch(p_lru, lru_conv_w[l], lru_conv_b[l], lru_wa[l], lru_ba[l],
                          lru_wx[l], lru_bx[l], lru_lam[l]),
            _hgrn2_branch(p_hg, lb_all[l], hg_norm[l]),
        )
        gates = jnp.split(jax.nn.sigmoid(p_gate), N_BRANCH, axis=-1)
        merged = gates[0] * (ys[0] @ w_branch[l, 0])
        for n in range(1, N_BRANCH):
            merged = merged + gates[n] * (ys[n] @ w_branch[l, n])
        h = h + gt1 * (merged @ w_out[l])
        u2 = _rmsnorm(h, norm2_g[l]) * (1.0 + sc2) + sh2
        h = h + gt2 * _conv_ffn(u2, ffn_up[l], ffn_conv_w[l], ffn_conv_b[l], ffn_down[l])
    return _rmsnorm(h, final_g)
```

```python
import functools
import math

import numpy as np
import jax
import jax.numpy as jnp
from jax import lax
from jax.experimental import pallas as pl
from jax.experimental.pallas import tpu as pltpu

F32 = jnp.float32
BF16 = jnp.bfloat16

D_MODEL = 1024
MIX_W = 512
N_BRANCH = 4
RW_N = 64
RW_H = 8
RW_GN_EPS = 64e-5
RW_COLS = 1792
ML_H = 4
ML_DV = 128
ML_DK = 64
ML_COLS_PAD = 1792
LRU_CONV = 4
RG_C = 8.0
HG_H = 4
HG_D = 128
D_FF = 2816
NEG_BIG = -1e30

CHUNK = 64
LRU_CHUNK = 256
ROW_TILE = 256
HALO = 8
FF_SPLIT = 2
VMEM_LIMIT = 56 * 1024 * 1024


def _mm(a, b):
    return jnp.dot(a.astype(BF16), b.astype(BF16), preferred_element_type=F32)


def _mm_nt(a, b):
    return lax.dot_general(a.astype(BF16), b.astype(BF16), (((1,), (1,)), ((), ())),
                           preferred_element_type=F32)


def _mm_tn(a, b):
    return lax.dot_general(a.astype(BF16), b.astype(BF16), (((0,), (0,)), ((), ())),
                           preferred_element_type=F32)


def _split3(x):
    hi = x.astype(BF16)
    r = x - hi.astype(F32)
    mid = r.astype(BF16)
    lo = (r - mid.astype(F32)).astype(BF16)
    return hi, mid, lo


def _mm_exact_l(sel, x):
    hi, mid, lo = _split3(x)
    d = lambda y: jnp.dot(sel, y, preferred_element_type=F32)
    return d(hi) + d(mid) + d(lo)


def _mm_exact_r(x, sel):
    hi, mid, lo = _split3(x)
    d = lambda y: jnp.dot(y, sel, preferred_element_type=F32)
    return d(hi) + d(mid) + d(lo)


def _sigmoid(x):
    return jax.nn.sigmoid(x)


def _softplus(x):
    return jnp.maximum(x, 0.0) + jnp.log(1.0 + jnp.exp(-jnp.abs(x)))


def _rmsnorm_rows(x, g, eps=1e-6):
    return x * lax.rsqrt(jnp.mean(x * x, axis=-1, keepdims=True) + eps) * g


def _iota(shape, dim):
    return lax.broadcasted_iota(jnp.int32, shape, dim)


def _shift_rows(x, prev_row, next_row, k):
    n = x.shape[0]
    row = _iota((n, 1), 0)
    rolled = pltpu.roll(x, (-k) % n, 0)
    if k == -1:
        return jnp.where(row == 0, prev_row, rolled)
    out = rolled
    for j in range(k):
        out = jnp.where(row == n - k + j, next_row[j:j + 1], out)
    return out


def _params(sem):
    return pltpu.CompilerParams(dimension_semantics=sem, vmem_limit_bytes=VMEM_LIMIT)


def _full(shape):
    nd = len(shape)
    return pl.BlockSpec(shape, lambda *_: (0,) * nd)


def _seq_views(width, rows, nsteps, col_block, halo):
    per = rows // HALO
    last = nsteps * per - 1
    specs = []
    for rev in (False, True):
        idx = (lambda c: nsteps - 1 - c) if rev else (lambda c: c)
        specs.append(pl.BlockSpec((1, rows, width), lambda b, c, idx=idx: (b, idx(c), col_block)))
        if halo:
            specs.append(pl.BlockSpec(
                (1, HALO, width), lambda b, c, idx=idx: (b, jnp.maximum(idx(c) * per - 1, 0), col_block)))
            specs.append(pl.BlockSpec(
                (1, HALO, width), lambda b, c, idx=idx: (b, jnp.minimum((idx(c) + 1) * per, last), col_block)))
    return specs


def _tri_masks(rev):
    row = _iota((CHUNK, 128), 0)
    col = jnp.bitwise_and(_iota((CHUNK, 128), 1), CHUNK - 1)
    if rev:
        strict, incl = col > row, col >= row
    else:
        strict, incl = col < row, col <= row
    blk = jnp.right_shift(col, 4) == jnp.right_shift(row, 4)
    return strict, incl, blk, col == row


def _bdr(x):
    lo = _iota(x.shape, 1) < 64
    return jnp.concatenate([jnp.where(lo, x, 0.0), jnp.where(lo, 0.0, x)], axis=0)


def _pmm(l, r):
    return _mm(l, _bdr(r))


def _ada_kernel(c_ref, w_ref, b_ref, o_ref):
    c = c_ref[...]
    cond = c * _sigmoid(c)
    o_ref[0, 0] = jnp.dot(cond, w_ref[0], preferred_element_type=F32,
                          precision=lax.Precision.HIGHEST) + b_ref[0, 0]


def _ada_mod(c, ada_w, ada_b):
    depth = ada_w.shape[0]
    bsz = c.shape[0]
    out = pl.pallas_call(
        _ada_kernel,
        grid=(depth, 6),
        in_specs=[_full((bsz, D_MODEL)),
                  pl.BlockSpec((1, D_MODEL, D_MODEL), lambda l, j: (l, 0, j)),
                  pl.BlockSpec((1, 1, 1, D_MODEL), lambda l, j: (l, j, 0, 0))],
        out_specs=pl.BlockSpec((1, 1, bsz, D_MODEL), lambda l, j: (l, j, 0, 0)),
        out_shape=jax.ShapeDtypeStruct((depth, 6, bsz, D_MODEL), F32),
        compiler_params=_params(("arbitrary", "arbitrary")),
        name="ada_mod",
    )(c, ada_w, ada_b.reshape(depth, 6, 1, D_MODEL))
    return out.reshape(depth, 6, bsz, 1, D_MODEL)


def _mod_spec(idx):
    return pl.BlockSpec((1, 1, 1, D_MODEL), lambda b, i: (idx, b, 0, 0))


def _proj_kernel(h_ref, g_ref, sc_ref, sh_ref, wrw, wml, wlru, whg, u_ref, prw, pml, plru, phg):
    u = _rmsnorm_rows(h_ref[0], g_ref[...]) * (1.0 + sc_ref[0, 0]) + sh_ref[0, 0]
    ub = u.astype(BF16)
    u_ref[0] = ub
    for w_ref, o_ref in ((wrw, prw), (wml, pml), (wlru, plru), (whg, phg)):
        o_ref[0] = jnp.dot(ub, w_ref[...], preferred_element_type=F32)


def _project(h, mod_l, g1, wrw, wml, wlru, whg):
    bsz, s, _ = h.shape
    widths = [w.shape[1] for w in (wrw, wml, wlru, whg)]
    row = lambda w: pl.BlockSpec((1, ROW_TILE, w), lambda b, i: (b, i, 0))
    return pl.pallas_call(
        _proj_kernel,
        grid=(bsz, s // ROW_TILE),
        in_specs=[row(D_MODEL), _full((1, D_MODEL)), _mod_spec(1), _mod_spec(0)]
        + [_full(w.shape) for w in (wrw, wml, wlru, whg)],
        out_specs=[row(D_MODEL)] + [row(w) for w in widths],
        out_shape=[jax.ShapeDtypeStruct((bsz, s, D_MODEL), BF16)]
        + [jax.ShapeDtypeStruct((bsz, s, w), F32) for w in widths],
        compiler_params=_params(("arbitrary", "arbitrary")),
        name="in_proj",
    )(h, g1, mod_l, mod_l, wrw, wml, wlru, whg)


def _rwkv_shifted(m_ref, p_ref, n_ref, first, last, mu):
    x = m_ref[0]
    pv = jnp.where(first, 0.0, p_ref[0, HALO - 1:HALO, :])
    nv = jnp.where(last, 0.0, n_ref[0, 0:1, :])
    prev = _shift_rows(x, pv, None, -1)
    nxt = _shift_rows(x, None, nv, 1)
    return x + mu * (0.5 * (prev + nxt) - x)


def _rwkv_lowrank(xs, d, w0a0_ref, wlr_ref):
    blk = xs[:, 1536:1664]
    lane = _iota(blk.shape, 1)
    lrin = jnp.where(lane < 64, jnp.tanh(blk), blk)
    return _mm(lrin, wlr_ref[d]) + w0a0_ref[d]


def _rwkv_dir(views, first, last, rev, d, mu_ref, w0a0_ref, wlr_ref, kk_ref, ka_ref, seg_ref, tri_ref,
              y_ref, s_ref):
    xs = _rwkv_shifted(*views, first, last, mu_ref[...])
    r, k, v = xs[:, 0:512], xs[:, 512:1024], xs[:, 1024:1536]
    lr = _rwkv_lowrank(xs, d, w0a0_ref, wlr_ref)
    lw = -jnp.exp(-_softplus(-lr[:, :MIX_W]) - 0.5)
    a = _sigmoid(lr[:, MIX_W:])
    kkm = k * kk_ref[...]
    ss = _mm(kkm * kkm, seg_ref[...])
    kkn = kkm * lax.rsqrt(jnp.maximum(ss, 1e-24))
    a_s, b_s = -kkn, kkn * a
    k_s = k * (1.0 + (a - 1.0) * ka_ref[...])

    cum = _mm_exact_l(tri_ref[d], lw)
    ctot = cum[0:1] if rev else cum[CHUNK - 1:CHUNK]
    e_neg = jnp.exp(-cum)
    e_hat = jnp.exp(ctot - cum)
    at, rt = a_s * jnp.exp(cum - lw), r * jnp.exp(cum)
    bt, kt = b_s * e_neg, k_s * e_neg
    bh, kh = b_s * e_hat, k_s * e_hat
    gam = jnp.exp(ctot)

    strict, incl, blk16, eye = _tri_masks(rev)
    brow = _iota((128, 128), 0) < 64
    bcol = _iota((128, 128), 1) < 64
    bdmask = brow == bcol
    for j in range(RW_H // 2):
        sl = slice(128 * j, 128 * (j + 1))
        vj = v[:, sl]
        ar = jnp.concatenate([at[:, sl], rt[:, sl]], axis=0)
        bk = jnp.concatenate([_bdr(bt[:, sl]), _bdr(kt[:, sl])], axis=0)
        sc = _mm_nt(ar, bk)
        a_ab = jnp.where(strict, sc[:CHUNK, :128], 0.0)
        a_ak = jnp.where(strict, sc[:CHUNK, 128:], 0.0)
        r_b = jnp.where(incl, sc[CHUNK:, :128], 0.0)
        r_k = jnp.where(incl, sc[CHUNK:, 128:], 0.0)
        a_d = jnp.where(blk16, a_ab, 0.0)
        a_o = a_ab - a_d
        a2 = _pmm(a_d, a_d)
        a4 = _pmm(a2, a2)
        a8 = _pmm(a4, a4)
        t = eye.astype(F32) + a_d
        t = t + _pmm(a2, t)
        t = t + _pmm(a4, t)
        t = t + _pmm(a8, t)
        nmat = _pmm(t, a_o)
        n2 = _pmm(nmat, nmat)
        t = t + _pmm(nmat, t)
        t = t + _pmm(n2, t)

        sbd = s_ref[j]
        ars = _mm_nt(ar, sbd)
        u = _pmm(t, ars[:CHUNK] + _pmm(a_ak, vj))
        y = ars[CHUNK:] + _mm(jnp.concatenate([r_b, r_k], axis=1),
                              jnp.concatenate([_bdr(u), _bdr(vj)], axis=0))
        y_ref[0, :, sl] = y
        upd = _mm_tn(jnp.concatenate([u, vj], axis=0),
                     jnp.concatenate([bh[:, sl], kh[:, sl]], axis=0))
        s_ref[j] = gam[:, sl] * sbd + jnp.where(bdmask, upd, 0.0)


def _rwkv_scan_kernel(nc, fm, fp, fn, bm, bp, bn, mu_ref, w0a0_ref, wlr_ref, kk_ref, ka_ref, seg_ref,
                      tri_ref, yf_ref, yb_ref, sf_ref, sb_ref):
    c = pl.program_id(1)

    @pl.when(c == 0)
    def _():
        sf_ref[...] = jnp.zeros_like(sf_ref)
        sb_ref[...] = jnp.zeros_like(sb_ref)

    common = (mu_ref, w0a0_ref, wlr_ref, kk_ref, ka_ref, seg_ref, tri_ref)
    _rwkv_dir((fm, fp, fn), c == 0, c == nc - 1, False, 0, *common, yf_ref, sf_ref)
    _rwkv_dir((bm, bp, bn), c == nc - 1, c == 0, True, 1, *common, yb_ref, sb_ref)


def _rwkv_scan(p_rw, prm):
    bsz, s, _ = p_rw.shape
    nc = s // CHUNK
    consts = (prm["mu"], prm["w0a0"], prm["wlr"], prm["kk"], prm["ka"], prm["seg64"], prm["tri"])
    out = lambda rev: pl.BlockSpec((1, CHUNK, MIX_W), (lambda b, c: (b, nc - 1 - c, 0)) if rev
                                   else (lambda b, c: (b, c, 0)))
    return pl.pallas_call(
        functools.partial(_rwkv_scan_kernel, nc),
        grid=(bsz, nc),
        in_specs=_seq_views(RW_COLS, CHUNK, nc, 0, True) + [_full(x.shape) for x in consts],
        out_specs=[out(False), out(True)],
        out_shape=[jax.ShapeDtypeStruct((bsz, s, MIX_W), F32)] * 2,
        scratch_shapes=[pltpu.VMEM((RW_H // 2, 128, 128), F32)] * 2,
        compiler_params=_params(("arbitrary", "arbitrary")),
        name="rwkv_scan",
    )(*([p_rw] * 6), *consts)


def _rwkv_fin_kernel(nt, pm, pp, pn, yf_ref, yb_ref, mu_ref, w0a0_ref, wlr_ref, ka_ref, rk_ref, lnw_ref,
                     lnb_ref, g2_ref, seg_ref, o_ref):
    i = pl.program_id(1)
    xs = _rwkv_shifted(pm, pp, pn, i == 0, i == nt - 1, mu_ref[...])
    r, k, v = xs[:, 0:512], xs[:, 512:1024], xs[:, 1024:1536]
    kts = []
    for d in range(2):
        a = _sigmoid(_rwkv_lowrank(xs, d, w0a0_ref, wlr_ref)[:, MIX_W:])
        kts.append(k * (1.0 + (a - 1.0) * ka_ref[...]))
    y = yf_ref[0] + yb_ref[0]
    inv_n = 1.0 / RW_N
    mean = _mm_exact_r(y, seg_ref[...]) * inv_n
    yc = y - mean
    var = _mm_exact_r(yc * yc, seg_ref[...]) * inv_n
    y = yc * lax.rsqrt(var + RW_GN_EPS) * lnw_ref[...] + lnb_ref[...]
    k_bonus = 0.5 * (kts[0] + kts[1])
    bonus = _mm_exact_r(r * k_bonus * rk_ref[...], seg_ref[...]) * v
    g = _mm(_sigmoid(xs[:, 1664:1792]), g2_ref[...])
    o_ref[0] = ((y + bonus) * g).astype(BF16)


def _rwkv_finalize(p_rw, yf, yb, prm):
    bsz, s, _ = p_rw.shape
    nt = s // ROW_TILE
    consts = (prm["mu"], prm["w0a0"], prm["wlr"], prm["ka"], prm["rk"], prm["lnw"], prm["lnb"], prm["g2"],
              prm["seg64"])
    row = pl.BlockSpec((1, ROW_TILE, MIX_W), lambda b, i: (b, i, 0))
    return pl.pallas_call(
        functools.partial(_rwkv_fin_kernel, nt),
        grid=(bsz, nt),
        in_specs=_seq_views(RW_COLS, ROW_TILE, nt, 0, True)[:3] + [row, row] + [_full(x.shape) for x in consts],
        out_specs=row,
        out_shape=jax.ShapeDtypeStruct((bsz, s, MIX_W), BF16),
        compiler_params=_params(("arbitrary", "arbitrary")),
        name="rwkv_finalize",
    )(p_rw, p_rw, p_rw, yf, yb, *consts)


def _mlstm_dir(qkv_ref, gate_ref, rev, d, bias_ref, tri_ref, h_ref, c_ref, n_ref, m_ref):
    qkv = qkv_ref[0]
    q = qkv[:, 0:256] * (ML_DK ** -0.5)
    k = qkv[:, 256:512]
    v = qkv[:, 512:1024]
    gates = gate_ref[0] + bias_ref[...]
    g_i = gates[:, 0:128]
    g_f = -_softplus(-gates[:, 128:256])
    bcum = _mm_exact_l(tri_ref[d], g_f)
    g_end = bcum[0:1] if rev else bcum[CHUNK - 1:CHUNK]
    a_end = g_end - bcum + g_i
    m_prev = m_ref[d]
    m_new = jnp.maximum(g_end + m_prev, jnp.max(a_end, axis=0, keepdims=True))
    w_c = jnp.exp(a_end - m_new)
    dec = jnp.exp(g_end + m_prev - m_new)
    m_inter = bcum + m_prev
    g_t = (g_i - bcum).T

    row = _iota((CHUNK, CHUNK), 0)
    col = _iota((CHUNK, CHUNK), 1)
    mask = (col >= row) if rev else (col <= row)
    lane_lo = _iota((CHUNK, 128), 1) < 64
    for pair in range(ML_H // 2):
        psl = slice(128 * pair, 128 * (pair + 1))
        qp, kp = q[:, psl], k[:, psl]
        cp = c_ref[d, pair]
        np_ = n_ref[d, :, psl]
        contrib = jnp.zeros((128, 128), F32)
        kw_sum = jnp.zeros((CHUNK, 128), F32)
        for half in range(2):
            hh = 2 * pair + half
            j = 4 * d + hh
            hm = lane_lo if half == 0 else jnp.logical_not(lane_lo)
            vh = v[:, 128 * hh:128 * (hh + 1)]
            qh = jnp.where(hm, qp, 0.0)
            log_d = jnp.where(mask, bcum[:, j:j + 1] + g_t[j:j + 1, :], NEG_BIG)
            m_t = jnp.maximum(m_inter[:, j:j + 1], jnp.max(log_d, axis=1, keepdims=True))
            pw = jnp.where(mask, jnp.exp(log_d - m_t), 0.0) * _mm_nt(qh, kp)
            s_int = jnp.exp(m_inter[:, j:j + 1] - m_t)
            num = s_int * _mm(qh, cp) + _mm(pw, vh)
            den = (s_int * jnp.sum(qh * np_, axis=1, keepdims=True)
                   + jnp.sum(pw, axis=1, keepdims=True))
            h_ref[0, :, 128 * hh:128 * (hh + 1)] = num / jnp.maximum(jnp.abs(den), jnp.exp(-m_t))
            kw = jnp.where(hm, kp, 0.0) * w_c[:, j:j + 1]
            contrib = contrib + _mm_tn(kw, vh)
            kw_sum = kw_sum + kw
        j0 = 4 * d + 2 * pair
        dec_rows = jnp.where(_iota((128, 1), 0) < 64, dec[:, j0:j0 + 1], dec[:, j0 + 1:j0 + 2])
        dec_lanes = jnp.where(_iota((1, 128), 1) < 64, dec[:, j0:j0 + 1], dec[:, j0 + 1:j0 + 2])
        c_ref[d, pair] = dec_rows * cp + contrib
        n_ref[d, :, psl] = dec_lanes * np_ + jnp.sum(kw_sum, axis=0, keepdims=True)
    m_ref[d] = m_new


def _mlstm_scan_kernel(fq, fg, bq, bg, bias_ref, tri_ref, hf_ref, hb_ref, c_ref, n_ref, m_ref):
    @pl.when(pl.program_id(1) == 0)
    def _():
        c_ref[...] = jnp.zeros_like(c_ref)
        n_ref[...] = jnp.zeros_like(n_ref)
        m_ref[...] = jnp.zeros_like(m_ref)

    _mlstm_dir(fq, fg, False, 0, bias_ref, tri_ref, hf_ref, c_ref, n_ref, m_ref)
    _mlstm_dir(bq, bg, True, 1, bias_ref, tri_ref, hb_ref, c_ref, n_ref, m_ref)


def _mlstm_scan(p_ml, prm):
    bsz, s, _ = p_ml.shape
    nc = s // CHUNK
    consts = (prm["ml_bias"], prm["tri"])
    fw = lambda w, cb: pl.BlockSpec((1, CHUNK, w), lambda b, c: (b, c, cb))
    bw = lambda w, cb: pl.BlockSpec((1, CHUNK, w), lambda b, c: (b, nc - 1 - c, cb))
    return pl.pallas_call(
        _mlstm_scan_kernel,
        grid=(bsz, nc),
        in_specs=[fw(1024, 0), fw(256, 6), bw(1024, 0), bw(256, 6)] + [_full(x.shape) for x in consts],
        out_specs=[fw(MIX_W, 0), bw(MIX_W, 0)],
        out_shape=[jax.ShapeDtypeStruct((bsz, s, MIX_W), F32)] * 2,
        scratch_shapes=[pltpu.VMEM((2, ML_H // 2, 128, 128), F32), pltpu.VMEM((2, 1, 256), F32),
                        pltpu.VMEM((2, 1, 128), F32)],
        compiler_params=_params(("arbitrary", "arbitrary")),
        name="mlstm_scan",
    )(p_ml, p_ml, p_ml, p_ml, *consts)


def _lru_dir(views, first, last, rev, d, cw_ref, cb_ref, wax_ref, bax_ref, lam_ref, h_ref, carry_ref,
             a_scr, b_scr):
    m_ref, p_ref, n_ref = views
    x = m_ref[0]
    pv = jnp.where(first, 0.0, p_ref[0, HALO - 1:HALO, :])
    nv = jnp.where(last, 0.0, n_ref[0, 0:2, :])
    cw = cw_ref[...]
    xc = (_shift_rows(x, pv, None, -1) * cw[0:1] + x * cw[1:2] + _shift_rows(x, None, nv, 1) * cw[2:3]
          + _shift_rows(x, None, nv, 2) * cw[3:4] + cb_ref[...])
    rr = _mm(xc, wax_ref[d]) + bax_ref[d]
    log_a = -RG_C * _softplus(-lam_ref[d]) * _sigmoid(rr[:, :MIX_W])
    a = jnp.exp(log_a)
    mult = jnp.sqrt(jnp.maximum(1.0 - jnp.exp(2.0 * log_a), 0.0))
    a_scr[...] = a
    b_scr[...] = mult * (_sigmoid(rr[:, MIX_W:]) * xc)

    ngroups = LRU_CHUNK // 8
    row8 = _iota((8, MIX_W), 0)

    def body(i, hprev):
        g = (ngroups - 1 - i) if rev else i
        r0 = pl.multiple_of(g * 8, 8)
        av = a_scr[pl.ds(r0, 8), :]
        bv = b_scr[pl.ds(r0, 8), :]
        for sh in (1, 2, 4):
            if rev:
                a_sh, b_sh = pltpu.roll(av, 8 - sh, 0), pltpu.roll(bv, 8 - sh, 0)
                valid = row8 < 8 - sh
            else:
                a_sh, b_sh = pltpu.roll(av, sh, 0), pltpu.roll(bv, sh, 0)
                valid = row8 >= sh
            bv = jnp.where(valid, av * b_sh + bv, bv)
            av = jnp.where(valid, av * a_sh, av)
        hv = bv + av * hprev
        h_ref[0, pl.ds(r0, 8), :] = hv
        return hv[0:1] if rev else hv[7:8]

    carry_ref[d] = lax.fori_loop(0, ngroups, body, carry_ref[d])


def _lru_scan_kernel(nc, fm, fp, fn, bm, bp, bn, cw_ref, cb_ref, wax_ref, bax_ref, lam_ref, hf_ref, hb_ref,
                     carry_ref, a_scr, b_scr):
    c = pl.program_id(1)

    @pl.when(c == 0)
    def _():
        carry_ref[...] = jnp.zeros_like(carry_ref)

    common = (cw_ref, cb_ref, wax_ref, bax_ref, lam_ref)
    _lru_dir((fm, fp, fn), c == 0, c == nc - 1, False, 0, *common, hf_ref, carry_ref, a_scr, b_scr)
    _lru_dir((bm, bp, bn), c == nc - 1, c == 0, True, 1, *common, hb_ref, carry_ref, a_scr, b_scr)


def _lru_scan(p_lru, prm):
    bsz, s, _ = p_lru.shape
    nc = s // LRU_CHUNK
    consts = (prm["lru_cw"], prm["lru_cb"], prm["lru_wax"], prm["lru_bax"], prm["lru_lam"])
    out = lambda rev: pl.BlockSpec((1, LRU_CHUNK, MIX_W), (lambda b, c: (b, nc - 1 - c, 0)) if rev
                                   else (lambda b, c: (b, c, 0)))
    return pl.pallas_call(
        functools.partial(_lru_scan_kernel, nc),
        grid=(bsz, nc),
        in_specs=_seq_views(MIX_W, LRU_CHUNK, nc, 0, True) + [_full(x.shape) for x in consts],
        out_specs=[out(False), out(True)],
        out_shape=[jax.ShapeDtypeStruct((bsz, s, MIX_W), F32)] * 2,
        scratch_shapes=[pltpu.VMEM((2, 1, MIX_W), F32), pltpu.VMEM((LRU_CHUNK, MIX_W), F32),
                        pltpu.VMEM((LRU_CHUNK, MIX_W), F32)],
        compiler_params=_params(("arbitrary", "arbitrary")),
        name="lru_scan",
    )(*([p_lru] * 6), *consts)


HG_LEVELS = (32, 16, 8, 4, 2, 1)


def _hgrn_lower_bound(lb_ref, layer, d):
    x = lb_ref[d]
    e = jnp.exp(x - jnp.max(x, axis=0, keepdims=True))
    sm = e / jnp.sum(e, axis=0, keepdims=True)
    acc = sm[0:1]
    for j in range(1, layer + 1):
        acc = acc + sm[j:j + 1]
    return jnp.clip(acc - sm[0:1], 0.0, 1.0)


def _hgrn_dir(q_ref, f_ref, i_ref, rev, d, layer, lb_ref, tri_ref, sel_ref, o_ref, s_ref):
    lb = _hgrn_lower_bound(lb_ref, layer, d)
    qraw, fp, v = q_ref[0], f_ref[0], i_ref[0]
    q = qraw * _sigmoid(qraw)
    logf = jnp.log(lb + (1.0 - lb) * _sigmoid(fp))
    kd = (1.0 - lb) * _sigmoid(-fp)
    gc = _mm_exact_l(tri_ref[d], logf)
    gtot = gc[0:1] if rev else gc[CHUNK - 1:CHUNK]
    q_in = q * jnp.exp(gc)
    k_out = kd * jnp.exp(gtot - gc)
    g_dec = jnp.exp(gtot)

    row = _iota((CHUNK, CHUNK), 0)
    col = _iota((CHUNK, CHUNK), 1)
    rowc = _iota((CHUNK, 1), 0)
    qs, ks, same = [], [], []
    for lv, half in enumerate(HG_LEVELS):
        ref_g = _mm_exact_l(sel_ref[d, lv], gc)
        in_hi = jnp.bitwise_and(rowc, 2 * half - 1) >= half
        is_q = jnp.logical_not(in_hi) if rev else in_hi
        qs.append(jnp.where(is_q, q * jnp.exp(jnp.minimum(gc - ref_g, 0.0)), 0.0))
        ks.append(jnp.where(is_q, 0.0, kd * jnp.exp(jnp.minimum(ref_g - gc, 0.0))))
        sh = int(math.log2(2 * half))
        same.append(jnp.right_shift(row, sh) == jnp.right_shift(col, sh))
    for hh in range(HG_H):
        sl = slice(128 * hh, 128 * (hh + 1))
        att = jnp.where(row == col, _mm_nt(q[:, sl], kd[:, sl]), 0.0)
        for lv in range(len(HG_LEVELS)):
            att = att + jnp.where(same[lv], _mm_nt(qs[lv][:, sl], ks[lv][:, sl]), 0.0)
        st = s_ref[d, hh]
        o_ref[0, :, sl] = _mm(att, v[:, sl]) + _mm_nt(q_in[:, sl], st)
        s_ref[d, hh] = st * g_dec[:, sl] + _mm_tn(v[:, sl], k_out[:, sl])


def _hgrn_scan_kernel(layer, fq, ff, fi, bq, bf, bi, lb_ref, tri_ref, sel_ref, of_ref, ob_ref, s_ref):
    @pl.when(pl.program_id(1) == 0)
    def _():
        s_ref[...] = jnp.zeros_like(s_ref)

    _hgrn_dir(fq, ff, fi, False, 0, layer, lb_ref, tri_ref, sel_ref, of_ref, s_ref)
    _hgrn_dir(bq, bf, bi, True, 1, layer, lb_ref, tri_ref, sel_ref, ob_ref, s_ref)


def _hgrn_scan(p_hg, hg_lb, layer, prm):
    bsz, s, _ = p_hg.shape
    nc = s // CHUNK
    consts = (hg_lb, prm["tri"], prm["hg_sel"])
    fw = lambda cb: pl.BlockSpec((1, CHUNK, MIX_W), lambda b, c: (b, c, cb))
    bw = lambda cb: pl.BlockSpec((1, CHUNK, MIX_W), lambda b, c: (b, nc - 1 - c, cb))
    return pl.pallas_call(
        functools.partial(_hgrn_scan_kernel, layer),
        grid=(bsz, nc),
        in_specs=[fw(0), fw(1), fw(3), bw(0), bw(2), bw(3)] + [_full(x.shape) for x in consts],
        out_specs=[fw(0), bw(0)],
        out_shape=[jax.ShapeDtypeStruct((bsz, s, MIX_W), F32)] * 2,
        scratch_shapes=[pltpu.VMEM((2, HG_H, HG_D, HG_D), F32)],
        compiler_params=_params(("arbitrary", "arbitrary")),
        name="hgrn_scan",
    )(*([p_hg] * 6), *consts)


def _head_rms128(x, g):
    parts = []
    for hh in range(MIX_W // 128):
        xh = x[:, 128 * hh:128 * (hh + 1)]
        parts.append(xh * lax.rsqrt(jnp.mean(xh * xh, axis=-1, keepdims=True) + 1e-6))
    return jnp.concatenate(parts, axis=1) * g


def _gelu_tanh(x):
    cdf = 0.5 * (1.0 + jnp.tanh(math.sqrt(2.0 / math.pi) * (x + 0.044715 * (x * x * x))))
    return x * cdf


def _merge_kernel(u_ref, h_ref, ya_ref, mlf, mlb, mlo, lrf, lrb, lrg, hgf, hgb, hgg, gt_ref, mln_ref,
                  hgn_ref, wg_ref, wb_ref, wo_ref, o_ref):
    gates = _sigmoid(jnp.dot(u_ref[0], wg_ref[...], preferred_element_type=F32))
    y_b = _sigmoid(mlo[0]) * _head_rms128(mlf[0] + mlb[0], mln_ref[...])
    y_c = (lrf[0] + lrb[0]) * _gelu_tanh(lrg[0])
    gg = hgg[0]
    y_d = (gg * _sigmoid(gg)) * _head_rms128(hgf[0] + hgb[0], hgn_ref[...])
    merged = None
    for n, y in enumerate((ya_ref[0], y_b, y_c, y_d)):
        term = gates[:, D_MODEL * n:D_MODEL * (n + 1)] * _mm(y, wb_ref[n])
        merged = term if merged is None else merged + term
    o_ref[0] = h_ref[0] + gt_ref[0, 0] * _mm(merged, wo_ref[...])


def _merge(u, h, y_a, ml, p_ml, lru, p_lru, hg, p_hg, mod_l, prm):
    bsz, s, _ = h.shape
    row = lambda w, cb=0: pl.BlockSpec((1, ROW_TILE, w), lambda b, i: (b, i, cb))
    consts = (prm["ml_norm"], prm["hg_norm"], prm["w_gate"], prm["w_branch"], prm["w_out"])
    return pl.pallas_call(
        _merge_kernel,
        grid=(bsz, s // ROW_TILE),
        in_specs=[row(D_MODEL), row(D_MODEL), row(MIX_W),
                  row(MIX_W), row(MIX_W), row(MIX_W, 2),
                  row(MIX_W), row(MIX_W), row(MIX_W, 1),
                  row(MIX_W), row(MIX_W), row(MIX_W, 4),
                  _mod_spec(2)] + [_full(x.shape) for x in consts],
        out_specs=row(D_MODEL),
        out_shape=jax.ShapeDtypeStruct((bsz, s, D_MODEL), F32),
        compiler_params=_params(("arbitrary", "arbitrary")),
        name="merge",
    )(u, h, y_a, ml[0], ml[1], p_ml, lru[0], lru[1], p_lru, hg[0], hg[1], p_hg, mod_l, *consts)


def _ffn_kernel(nt, final, hm, hp, hn, g_ref, sc_ref, sh_ref, gt_ref, wup_ref, cw_ref, cb_ref, wdn_ref,
                fg_ref, o_ref):
    i = pl.program_id(1)
    hx = jnp.concatenate([hp[0], hm[0], hn[0]], axis=0)
    n = hx.shape[0]
    u2 = (_rmsnorm_rows(hx, g_ref[...]) * (1.0 + sc_ref[0, 0]) + sh_ref[0, 0]).astype(BF16)
    row = _iota((n, 1), 0)
    valid = jnp.logical_and(jnp.logical_or(row >= HALO, i > 0),
                            jnp.logical_or(row < n - HALO, i < nt - 1))
    cw = cw_ref[...]
    cb = cb_ref[...]
    fc = D_FF // FF_SPLIT
    acc = None
    for j in range(FF_SPLIT):
        zs = []
        for base in (0, D_FF):
            sl = slice(base + fc * j, base + fc * (j + 1))
            y = jnp.where(valid, jnp.dot(u2, wup_ref[:, sl], preferred_element_type=F32), 0.0)
            z = (pltpu.roll(y, 1, 0) * cw[0:1, sl] + y * cw[1:2, sl] + pltpu.roll(y, n - 1, 0) * cw[2:3, sl]
                 + cb[:, sl])
            zs.append(z[HALO:n - HALO])
        val, gate = zs
        act = val * (gate * _sigmoid(gate))
        part = _mm(act, wdn_ref[fc * j:fc * (j + 1), :])
        acc = part if acc is None else acc + part
    out = hm[0] + gt_ref[0, 0] * acc
    if final:
        out = _rmsnorm_rows(out, fg_ref[...])
    o_ref[0] = out


def _conv_ffn(h, mod_l, prm, final_g, final):
    bsz, s, _ = h.shape
    nt = s // ROW_TILE
    consts = (prm["ffn_up"], prm["ffn_cw"], prm["ffn_cb"], prm["ffn_down"], final_g)
    row = pl.BlockSpec((1, ROW_TILE, D_MODEL), lambda b, i: (b, i, 0))
    return pl.pallas_call(
        functools.partial(_ffn_kernel, nt, final),
        grid=(bsz, nt),
        in_specs=_seq_views(D_MODEL, ROW_TILE, nt, 0, True)[:3]
        + [_full((1, D_MODEL)), _mod_spec(4), _mod_spec(3), _mod_spec(5)] + [_full(x.shape) for x in consts],
        out_specs=row,
        out_shape=jax.ShapeDtypeStruct((bsz, s, D_MODEL), F32),
        compiler_params=_params(("arbitrary", "arbitrary")),
        name="conv_ffn",
    )(h, h, h, prm["norm2_g"], mod_l, mod_l, mod_l, *consts)


def _block_diag(blocks):
    g, n, m = blocks.shape
    eye = jnp.eye(g, dtype=blocks.dtype)
    return (eye[:, None, :, None] * blocks[:, :, None, :]).reshape(g * n, g * m)


def _constants():
    idx = np.arange(CHUNK)
    tri = np.stack([idx[:, None] >= idx[None, :], idx[:, None] <= idx[None, :]]).astype(np.float32)
    sel = np.zeros((2, len(HG_LEVELS), CHUNK, CHUNK), np.float32)
    for lv, half in enumerate(HG_LEVELS):
        start = (idx // (2 * half)) * (2 * half)
        sel[0, lv, idx, start + half - 1] = 1.0
        sel[1, lv, idx, start + half] = 1.0
    lane = np.arange(MIX_W)
    seg = (lane[:, None] // RW_N == lane[None, :] // RW_N).astype(np.float32)
    return (jnp.asarray(tri, BF16), jnp.asarray(sel, BF16), jnp.asarray(seg, BF16))


def _layer_params(l, tri, sel, seg, w_in, rw_mu, rw_w0, rw_w2, rw_a0, rw_a2, rw_g2, rw_kk, rw_ka, rw_rk,
                  rw_lnw, rw_lnb, ml_ibias, ml_fbias, ml_norm, lru_conv_w, lru_conv_b, lru_wa, lru_ba,
                  lru_wx, lru_bx, lru_lam, hg_norm, w_branch, w_out, norm2_g, ffn_up, ffn_conv_w,
                  ffn_conv_b, ffn_down):
    row = lambda x: x.reshape(1, -1)
    w = w_in[l]
    o_ml = RW_COLS
    o_lru = o_ml + 1552
    o_hg = o_lru + 2 * MIX_W
    o_gate = o_hg + 5 * MIX_W
    zeros = jnp.zeros((D_MODEL, 120), F32)
    w_ml = jnp.concatenate([w[:, o_ml:o_ml + 1536], w[:, o_ml + 1536:o_ml + 1544], zeros,
                            w[:, o_ml + 1544:o_ml + 1552], zeros], axis=1)
    z64 = jnp.zeros((64, MIX_W), F32)
    wlr = jnp.stack([jnp.concatenate([jnp.concatenate([rw_w2[l, d], z64], axis=1),
                                      jnp.concatenate([z64, rw_a2[l, d]], axis=1)], axis=0) for d in range(2)])
    pad = jnp.zeros((120,), F32)
    ml_bias = jnp.concatenate([ml_ibias[l].reshape(-1), pad, ml_fbias[l].reshape(-1), pad]).reshape(1, 256)
    wax = jnp.stack([jnp.concatenate([_block_diag(lru_wa[l, d]), _block_diag(lru_wx[l, d])], axis=1)
                     for d in range(2)])
    return {
        "tri": tri, "hg_sel": sel, "seg64": seg,
        "w_rw": w[:, :RW_COLS].astype(BF16), "w_ml": w_ml.astype(BF16),
        "w_lru": w[:, o_lru:o_hg].astype(BF16), "w_hg": w[:, o_hg:o_gate].astype(BF16),
        "w_gate": w[:, o_gate:].astype(BF16),
        "mu": row(rw_mu[l]),
        "w0a0": jnp.concatenate([rw_w0[l], rw_a0[l]], axis=1).reshape(2, 1, 2 * MIX_W),
        "wlr": wlr.astype(BF16), "kk": row(rw_kk[l]), "ka": row(rw_ka[l]), "rk": row(rw_rk[l]),
        "lnw": row(rw_lnw[l]), "lnb": row(rw_lnb[l]), "g2": rw_g2[l].astype(BF16),
        "ml_bias": ml_bias, "ml_norm": row(ml_norm[l]),
        "lru_cw": lru_conv_w[l], "lru_cb": row(lru_conv_b[l]), "lru_wax": wax.astype(BF16),
        "lru_bax": jnp.concatenate([lru_ba[l], lru_bx[l]], axis=1).reshape(2, 1, 2 * MIX_W),
        "lru_lam": lru_lam[l].reshape(2, 1, MIX_W),
        "hg_norm": row(hg_norm[l]),
        "w_branch": w_branch[l].astype(BF16), "w_out": w_out[l].astype(BF16),
        "norm2_g": row(norm2_g[l]), "ffn_up": ffn_up[l].astype(BF16), "ffn_cw": ffn_conv_w[l],
        "ffn_cb": row(ffn_conv_b[l]), "ffn_down": ffn_down[l].astype(BF16),
    }


def kernel(x, c, ada_w, ada_b, norm1_g, w_in, rw_mu, rw_w0, rw_w2, rw_a0, rw_a2, rw_g2, rw_kk, rw_ka, rw_rk, rw_lnw, rw_lnb, ml_ibias, ml_fbias, ml_norm, lru_conv_w, lru_conv_b, lru_wa, lru_ba, lru_wx, lru_bx, lru_lam, hg_lb, hg_norm, w_branch, w_out, norm2_g, ffn_up, ffn_conv_w, ffn_conv_b, ffn_down, final_g):
    depth = w_in.shape[0]
    assert x.shape[1] % LRU_CHUNK == 0 and x.shape[2] == D_MODEL
    tri, sel, seg = _constants()
    mod = _ada_mod(c.astype(F32), ada_w, ada_b)
    h = x.astype(F32)
    final_row = final_g.reshape(1, D_MODEL)
    hg_lb = jnp.swapaxes(hg_lb.astype(F32), 0, 1)
    for l in range(depth):
        prm = _layer_params(l, tri, sel, seg, w_in, rw_mu, rw_w0, rw_w2, rw_a0, rw_a2, rw_g2, rw_kk, rw_ka,
                            rw_rk, rw_lnw, rw_lnb, ml_ibias, ml_fbias, ml_norm, lru_conv_w, lru_conv_b,
                            lru_wa, lru_ba, lru_wx, lru_bx, lru_lam, hg_norm, w_branch, w_out, norm2_g,
                            ffn_up, ffn_conv_w, ffn_conv_b, ffn_down)
        mod_l = mod[l]
        u, p_rw, p_ml, p_lru, p_hg = _project(h, mod_l, norm1_g[l].reshape(1, D_MODEL), prm["w_rw"],
                                              prm["w_ml"], prm["w_lru"], prm["w_hg"])
        y_a = _rwkv_finalize(p_rw, *_rwkv_scan(p_rw, prm), prm)
        ml = _mlstm_scan(p_ml, prm)
        lru = _lru_scan(p_lru, prm)
        hg = _hgrn_scan(p_hg, hg_lb, l, prm)
        h = _merge(u, h, y_a, ml, p_ml, lru, p_lru, hg, p_hg, mod_l, prm)
        h = _conv_ffn(h, mod_l, prm, final_row, l == depth - 1)
    return h
```

```python
import functools
import math

import numpy as np
import jax
import jax.numpy as jnp
from jax import lax
from jax.experimental import pallas as pl
from jax.experimental.pallas import tpu as pltpu

F32 = jnp.float32
BF16 = jnp.bfloat16

D_MODEL = 1024
MIX_W = 512
N_BRANCH = 4
RW_N = 64
RW_H = 8
RW_GN_EPS = 64e-5
RW_COLS = 1792
ML_H = 4
ML_DV = 128
ML_DK = 64
ML_COLS_PAD = 1792
LRU_CONV = 4
RG_C = 8.0
HG_H = 4
HG_D = 128
D_FF = 2816
NEG_BIG = -1e30

CHUNK = 64
SCAN_ROWS = 256
LRU_CHUNK = 256
ROW_TILE = 256
HALO = 8
FF_SPLIT = 2
VMEM_LIMIT = 56 * 1024 * 1024


def _mm(a, b):
    return jnp.dot(a.astype(BF16), b.astype(BF16), preferred_element_type=F32)


def _mm_nt(a, b):
    return lax.dot_general(a.astype(BF16), b.astype(BF16), (((1,), (1,)), ((), ())),
                           preferred_element_type=F32)


def _mm_tn(a, b):
    return lax.dot_general(a.astype(BF16), b.astype(BF16), (((0,), (0,)), ((), ())),
                           preferred_element_type=F32)


def _split3(x):
    hi = x.astype(BF16)
    r = x - hi.astype(F32)
    mid = r.astype(BF16)
    lo = (r - mid.astype(F32)).astype(BF16)
    return hi, mid, lo


def _mm_exact_l(sel, x):
    n = x.shape[1]
    r = jnp.dot(sel, jnp.concatenate(_split3(x), axis=1), preferred_element_type=F32)
    return r[:, :n] + r[:, n:2 * n] + r[:, 2 * n:]


def _mm_exact_r(x, sel):
    m = x.shape[0]
    r = jnp.dot(jnp.concatenate(_split3(x), axis=0), sel, preferred_element_type=F32)
    return r[:m] + r[m:2 * m] + r[2 * m:]


def _sigmoid(x):
    return jax.nn.sigmoid(x)


def _softplus(x):
    return jnp.maximum(x, 0.0) + jnp.log(1.0 + jnp.exp(-jnp.abs(x)))


def _rmsnorm_rows(x, g, eps=1e-6):
    return x * lax.rsqrt(jnp.mean(x * x, axis=-1, keepdims=True) + eps) * g


def _iota(shape, dim):
    return lax.broadcasted_iota(jnp.int32, shape, dim)


def _shift_rows(x, prev_row, next_row, k):
    n = x.shape[0]
    row = _iota((n, 1), 0)
    rolled = pltpu.roll(x, (-k) % n, 0)
    if k == -1:
        return jnp.where(row == 0, prev_row, rolled)
    out = rolled
    for j in range(k):
        out = jnp.where(row == n - k + j, next_row[j:j + 1], out)
    return out


def _params(sem):
    return pltpu.CompilerParams(dimension_semantics=sem, vmem_limit_bytes=VMEM_LIMIT)


def _full(shape):
    nd = len(shape)
    return pl.BlockSpec(shape, lambda *_: (0,) * nd)


def _seq_views(width, rows, nsteps, col_block, halo):
    per = rows // HALO
    last = nsteps * per - 1
    specs = []
    for rev in (False, True):
        idx = (lambda c: nsteps - 1 - c) if rev else (lambda c: c)
        specs.append(pl.BlockSpec((1, rows, width), lambda b, c, idx=idx: (b, idx(c), col_block)))
        if halo:
            specs.append(pl.BlockSpec(
                (1, HALO, width), lambda b, c, idx=idx: (b, jnp.maximum(idx(c) * per - 1, 0), col_block)))
            specs.append(pl.BlockSpec(
                (1, HALO, width), lambda b, c, idx=idx: (b, jnp.minimum((idx(c) + 1) * per, last), col_block)))
    return specs


def _tri_masks(rev):
    row = _iota((CHUNK, 128), 0)
    col = jnp.bitwise_and(_iota((CHUNK, 128), 1), CHUNK - 1)
    if rev:
        strict, incl = col > row, col >= row
    else:
        strict, incl = col < row, col <= row
    blk = jnp.right_shift(col, 4) == jnp.right_shift(row, 4)
    return strict, incl, blk, col == row


def _bdr(x):
    lo = _iota(x.shape, 1) < 64
    return jnp.concatenate([jnp.where(lo, x, 0.0), jnp.where(lo, 0.0, x)], axis=0)


def _pmm(l, r):
    return _mm(l, _bdr(r))


def _ada_kernel(c_ref, w_ref, b_ref, o_ref):
    c = c_ref[...]
    cond = c * _sigmoid(c)
    o_ref[0, 0] = jnp.dot(cond, w_ref[0], preferred_element_type=F32,
                          precision=lax.Precision.HIGHEST) + b_ref[0, 0]


def _ada_mod(c, ada_w, ada_b):
    depth = ada_w.shape[0]
    bsz = c.shape[0]
    out = pl.pallas_call(
        _ada_kernel,
        grid=(depth, 6),
        in_specs=[_full((bsz, D_MODEL)),
                  pl.BlockSpec((1, D_MODEL, D_MODEL), lambda l, j: (l, 0, j)),
                  pl.BlockSpec((1, 1, 1, D_MODEL), lambda l, j: (l, j, 0, 0))],
        out_specs=pl.BlockSpec((1, 1, bsz, D_MODEL), lambda l, j: (l, j, 0, 0)),
        out_shape=jax.ShapeDtypeStruct((depth, 6, bsz, D_MODEL), F32),
        compiler_params=_params(("arbitrary", "arbitrary")),
        name="ada_mod",
    )(c, ada_w, ada_b.reshape(depth, 6, 1, D_MODEL))
    return out.reshape(depth, 6, bsz, 1, D_MODEL)


def _mod_spec(idx):
    return pl.BlockSpec((1, 1, 1, D_MODEL), lambda b, i: (idx, b, 0, 0))


def _proj_kernel(h_ref, g_ref, sc_ref, sh_ref, wrw, wml, wlru, whg, u_ref, prw, pml, plru, phg):
    u = _rmsnorm_rows(h_ref[0], g_ref[...]) * (1.0 + sc_ref[0, 0]) + sh_ref[0, 0]
    ub = u.astype(BF16)
    u_ref[0] = ub
    for w_ref, o_ref in ((wrw, prw), (wml, pml), (wlru, plru), (whg, phg)):
        o_ref[0] = jnp.dot(ub, w_ref[...], preferred_element_type=F32)


def _project(h, mod_l, g1, wrw, wml, wlru, whg):
    bsz, s, _ = h.shape
    widths = [w.shape[1] for w in (wrw, wml, wlru, whg)]
    row = lambda w: pl.BlockSpec((1, ROW_TILE, w), lambda b, i: (b, i, 0))
    return pl.pallas_call(
        _proj_kernel,
        grid=(bsz, s // ROW_TILE),
        in_specs=[row(D_MODEL), _full((1, D_MODEL)), _mod_spec(1), _mod_spec(0)]
        + [_full(w.shape) for w in (wrw, wml, wlru, whg)],
        out_specs=[row(D_MODEL)] + [row(w) for w in widths],
        out_shape=[jax.ShapeDtypeStruct((bsz, s, D_MODEL), BF16)]
        + [jax.ShapeDtypeStruct((bsz, s, w), F32) for w in widths],
        compiler_params=_params(("arbitrary", "arbitrary")),
        name="in_proj",
    )(h, g1, mod_l, mod_l, wrw, wml, wlru, whg)


def _rwkv_shifted(m_ref, p_ref, n_ref, first, last, mu):
    x = m_ref[0]
    pv = jnp.where(first, 0.0, p_ref[0, HALO - 1:HALO, :])
    nv = jnp.where(last, 0.0, n_ref[0, 0:1, :])
    prev = _shift_rows(x, pv, None, -1)
    nxt = _shift_rows(x, None, nv, 1)
    return x + mu * (0.5 * (prev + nxt) - x)


def _rwkv_lowrank(xs, d, w0a0_ref, wlr_ref):
    blk = xs[:, 1536:1664]
    lane = _iota(blk.shape, 1)
    lrin = jnp.where(lane < 64, jnp.tanh(blk), blk)
    return _mm(lrin, wlr_ref[d]) + w0a0_ref[d]


def _rwkv_dir(views, first, last, rev, d, mu_ref, w0a0_ref, wlr_ref, kk_ref, ka_ref, seg_ref, tri_ref):
    xs = _rwkv_shifted(*views, first, last, mu_ref[...])
    r, k, v = xs[:, 0:512], xs[:, 512:1024], xs[:, 1024:1536]
    lr = _rwkv_lowrank(xs, d, w0a0_ref, wlr_ref)
    lw = -jnp.exp(-_softplus(-lr[:, :MIX_W]) - 0.5)
    a = _sigmoid(lr[:, MIX_W:])
    kkm = k * kk_ref[...]
    ss = _mm(kkm * kkm, seg_ref[...])
    kkn = kkm * lax.rsqrt(jnp.maximum(ss, 1e-24))
    a_s, b_s = -kkn, kkn * a
    k_s = k * (1.0 + (a - 1.0) * ka_ref[...])

    cum = _mm_exact_l(tri_ref[d], lw)
    nchunk = xs.shape[0] // CHUNK
    tot_rows = [cum[CHUNK * i:CHUNK * i + 1] if rev else cum[CHUNK * (i + 1) - 1:CHUNK * (i + 1)]
                for i in range(nchunk)]
    ctot = jnp.concatenate([jnp.broadcast_to(t, (CHUNK, MIX_W)) for t in tot_rows], axis=0)
    e_neg = jnp.exp(-cum)
    e_hat = jnp.exp(ctot - cum)
    at, rt = a_s * jnp.exp(cum - lw), r * jnp.exp(cum)
    bt, kt = b_s * e_neg, k_s * e_neg
    bh, kh = b_s * e_hat, k_s * e_hat

    strict, incl, blk16, eye = _tri_masks(rev)
    brow = _iota((128, 128), 0) < 64
    bcol = _iota((128, 128), 1) < 64
    bdmask = brow == bcol
    masks = (strict, incl, blk16, eye, bdmask)
    chains = []
    for i in range(nchunk):
        rs = slice(CHUNK * i, CHUNK * (i + 1))
        gam = jnp.exp(tot_rows[i])
        chains.append([_rwkv_pair_free(at[rs, sl], rt[rs, sl], bt[rs, sl], kt[rs, sl], bh[rs, sl], kh[rs, sl],
                                       v[rs, sl], gam[:, sl], masks)
                       for sl in (slice(128 * j, 128 * (j + 1)) for j in range(RW_H // 2))])
    return chains


def _rwkv_pair_free(at, rt, bt, kt, bh, kh, vj, gam, masks):
    strict, incl, blk16, eye, bdmask = masks
    ar = jnp.concatenate([at, rt], axis=0)
    bk = jnp.concatenate([_bdr(bt), _bdr(kt)], axis=0)
    sc = _mm_nt(ar, bk)
    yield
    a_ab = jnp.where(strict, sc[:CHUNK, :128], 0.0)
    a_ak = jnp.where(strict, sc[:CHUNK, 128:], 0.0)
    r_b = jnp.where(incl, sc[CHUNK:, :128], 0.0)
    r_k = jnp.where(incl, sc[CHUNK:, 128:], 0.0)
    a_d = jnp.where(blk16, a_ab, 0.0)
    a_o = a_ab - a_d
    a2 = _pmm(a_d, a_d)
    avk = _pmm(a_ak, vj)
    yield
    t = eye.astype(F32) + a_d
    a4 = _pmm(a2, a2)
    t = t + _pmm(a2, t)
    yield
    a8 = _pmm(a4, a4)
    t = t + _pmm(a4, t)
    yield
    t = t + _pmm(a8, t)
    yield
    nmat = _pmm(t, a_o)
    yield
    n2 = _pmm(nmat, nmat)
    t = t + _pmm(nmat, t)
    yield
    t = t + _pmm(n2, t)
    yield
    return dict(ar=ar, t=t, avk=avk, rbk=jnp.concatenate([r_b, r_k], axis=1), vj=vj,
                bhk=jnp.concatenate([bh, kh], axis=0), gam=gam, bdmask=bdmask)


def _rwkv_pair_carry(free, sbd):
    ars = _mm_nt(free["ar"], sbd)
    yield
    u = _pmm(free["t"], ars[:CHUNK] + free["avk"])
    yield
    vj = free["vj"]
    y = ars[CHUNK:] + _mm(free["rbk"], jnp.concatenate([_bdr(u), _bdr(vj)], axis=0))
    upd = _mm_tn(jnp.concatenate([u, vj], axis=0), free["bhk"])
    return y, free["gam"] * sbd + jnp.where(free["bdmask"], upd, 0.0)


def _run_round_robin(chains):
    results = [None] * len(chains)
    active = list(range(len(chains)))
    while active:
        for i in list(active):
            try:
                next(chains[i])
            except StopIteration as stop:
                results[i] = stop.value
                active.remove(i)
    return results


def _rwkv_scan_kernel(nc, fm, fp, fn, bm, bp, bn, mu_ref, w0a0_ref, wlr_ref, kk_ref, ka_ref, seg_ref,
                      tri_ref, yf_ref, yb_ref, sf_ref, sb_ref):
    c = pl.program_id(1)
    npair = RW_H // 2
    nchunk = SCAN_ROWS // CHUNK
    states = [[jnp.where(c == 0, 0.0, s_ref[j]) for j in range(npair)] for s_ref in (sf_ref, sb_ref)]
    common = (mu_ref, w0a0_ref, wlr_ref, kk_ref, ka_ref, seg_ref, tri_ref)
    free_f = _rwkv_dir((fm, fp, fn), c == 0, c == nc - 1, False, 0, *common)
    free_b = _rwkv_dir((bm, bp, bn), c == nc - 1, c == 0, True, 1, *common)
    res = _run_round_robin([g for chunk in free_f + free_b for g in chunk])
    free = [res[:nchunk * npair], res[nchunk * npair:]]
    y_refs = (yf_ref, yb_ref)
    outs = []
    for step in range(nchunk):
        chunk_of = (step, nchunk - 1 - step)
        res = _run_round_robin([_rwkv_pair_carry(free[dd][chunk_of[dd] * npair + j], states[dd][j])
                                for dd in range(2) for j in range(npair)])
        for dd in range(2):
            for j in range(npair):
                y, states[dd][j] = res[dd * npair + j]
                outs.append((dd, chunk_of[dd], j, y))
    for dd, i, j, y in outs:
        y_refs[dd][0, CHUNK * i:CHUNK * (i + 1), 128 * j:128 * (j + 1)] = y
    for dd, s_ref in enumerate((sf_ref, sb_ref)):
        for j in range(npair):
            s_ref[j] = states[dd][j]


def _rwkv_scan(p_rw, prm):
    bsz, s, _ = p_rw.shape
    nc = s // SCAN_ROWS
    consts = (prm["mu"], prm["w0a0"], prm["wlr"], prm["kk"], prm["ka"], prm["seg64"], prm["trib"])
    out = lambda rev: pl.BlockSpec((1, SCAN_ROWS, MIX_W), (lambda b, c: (b, nc - 1 - c, 0)) if rev
                                   else (lambda b, c: (b, c, 0)))
    return pl.pallas_call(
        functools.partial(_rwkv_scan_kernel, nc),
        grid=(bsz, nc),
        in_specs=_seq_views(RW_COLS, SCAN_ROWS, nc, 0, True) + [_full(x.shape) for x in consts],
        out_specs=[out(False), out(True)],
        out_shape=[jax.ShapeDtypeStruct((bsz, s, MIX_W), F32)] * 2,
        scratch_shapes=[pltpu.VMEM((RW_H // 2, 128, 128), F32)] * 2,
        compiler_params=_params(("arbitrary", "arbitrary")),
        name="rwkv_scan",
    )(*([p_rw] * 6), *consts)


def _rwkv_fin_kernel(nt, pm, pp, pn, yf_ref, yb_ref, mu_ref, w0a0_ref, wlr_ref, ka_ref, rk_ref, lnw_ref,
                     lnb_ref, g2_ref, seg_ref, o_ref):
    i = pl.program_id(1)
    xs = _rwkv_shifted(pm, pp, pn, i == 0, i == nt - 1, mu_ref[...])
    r, k, v = xs[:, 0:512], xs[:, 512:1024], xs[:, 1024:1536]
    kts = []
    for d in range(2):
        a = _sigmoid(_rwkv_lowrank(xs, d, w0a0_ref, wlr_ref)[:, MIX_W:])
        kts.append(k * (1.0 + (a - 1.0) * ka_ref[...]))
    y = yf_ref[0] + yb_ref[0]
    inv_n = 1.0 / RW_N
    mean = _mm_exact_r(y, seg_ref[...]) * inv_n
    yc = y - mean
    var = _mm_exact_r(yc * yc, seg_ref[...]) * inv_n
    y = yc * lax.rsqrt(var + RW_GN_EPS) * lnw_ref[...] + lnb_ref[...]
    k_bonus = 0.5 * (kts[0] + kts[1])
    bonus = _mm_exact_r(r * k_bonus * rk_ref[...], seg_ref[...]) * v
    g = _mm(_sigmoid(xs[:, 1664:1792]), g2_ref[...])
    o_ref[0] = ((y + bonus) * g).astype(BF16)


def _rwkv_finalize(p_rw, yf, yb, prm):
    bsz, s, _ = p_rw.shape
    nt = s // ROW_TILE
    consts = (prm["mu"], prm["w0a0"], prm["wlr"], prm["ka"], prm["rk"], prm["lnw"], prm["lnb"], prm["g2"],
              prm["seg64"])
    row = pl.BlockSpec((1, ROW_TILE, MIX_W), lambda b, i: (b, i, 0))
    return pl.pallas_call(
        functools.partial(_rwkv_fin_kernel, nt),
        grid=(bsz, nt),
        in_specs=_seq_views(RW_COLS, ROW_TILE, nt, 0, True)[:3] + [row, row] + [_full(x.shape) for x in consts],
        out_specs=row,
        out_shape=jax.ShapeDtypeStruct((bsz, s, MIX_W), BF16),
        compiler_params=_params(("arbitrary", "arbitrary")),
        name="rwkv_finalize",
    )(p_rw, p_rw, p_rw, yf, yb, *consts)


def _mlstm_dir(qkv_ref, gate_ref, rev, d, bias_ref, tri_ref, c_prev, n_prev, m_prev):
    qkv = qkv_ref[0]
    nchunk = qkv.shape[0] // CHUNK
    q = qkv[:, 0:256] * (ML_DK ** -0.5)
    k = qkv[:, 256:512]
    v = qkv[:, 512:1024]
    gates = gate_ref[0] + bias_ref[...]
    g_i = gates[:, 0:128]
    g_f = -_softplus(-gates[:, 128:256])
    bcum = _mm_exact_l(tri_ref[d], g_f)
    order = range(nchunk - 1, -1, -1) if rev else range(nchunk)
    rows = [slice(CHUNK * i, CHUNK * (i + 1)) for i in range(nchunk)]

    m = m_prev
    m_before, w_c, dec = [None] * nchunk, [None] * nchunk, [None] * nchunk
    for i in order:
        g_end = bcum[CHUNK * i:CHUNK * i + 1] if rev else bcum[CHUNK * (i + 1) - 1:CHUNK * (i + 1)]
        a_end = g_end - bcum[rows[i]] + g_i[rows[i]]
        m_new = jnp.maximum(g_end + m, jnp.max(a_end, axis=0, keepdims=True))
        w_c[i] = jnp.exp(a_end - m_new)
        dec[i] = jnp.exp(g_end + m - m_new)
        m_before[i] = m
        m = m_new

    row = _iota((CHUNK, CHUNK), 0)
    col = _iota((CHUNK, CHUNK), 1)
    mask = (col >= row) if rev else (col <= row)
    lane_lo = _iota((CHUNK, 128), 1) < 64
    heads = [(hh,) + divmod(hh, 2) for hh in range(ML_H)]

    qh, qk, contrib, kw = {}, {}, {}, {}
    for i in range(nchunk):
        for hh, pair, half in heads:
            psl = slice(128 * pair, 128 * (pair + 1))
            j = 4 * d + hh
            hm = lane_lo if half == 0 else jnp.logical_not(lane_lo)
            kp = k[rows[i], psl]
            qh[i, hh] = jnp.where(hm, q[rows[i], psl], 0.0)
            kw[i, hh] = jnp.where(hm, kp, 0.0) * w_c[i][:, j:j + 1]
    for i in range(nchunk):
        for hh, pair, half in heads:
            qk[i, hh] = _mm_nt(qh[i, hh], k[rows[i], 128 * pair:128 * (pair + 1)])
            contrib[i, hh] = _mm_tn(kw[i, hh], v[rows[i], 128 * hh:128 * (hh + 1)])

    c_cur, n_cur = list(c_prev), n_prev
    c_before, n_before = {}, {}
    for i in order:
        n_before[i] = n_cur
        n_parts = []
        for pair in range(ML_H // 2):
            j0 = 4 * d + 2 * pair
            d0, d1 = dec[i][:, j0:j0 + 1], dec[i][:, j0 + 1:j0 + 2]
            c_before[i, pair] = c_cur[pair]
            c_cur[pair] = (jnp.where(_iota((128, 1), 0) < 64, d0, d1) * c_cur[pair]
                           + (contrib[i, 2 * pair] + contrib[i, 2 * pair + 1]))
            n_parts.append(jnp.where(_iota((1, 128), 1) < 64, d0, d1) * n_cur[:, 128 * pair:128 * (pair + 1)]
                           + jnp.sum(kw[i, 2 * pair] + kw[i, 2 * pair + 1], axis=0, keepdims=True))
        n_cur = jnp.concatenate(n_parts, axis=1)

    keys = [(i, hh, pair) for i in range(nchunk) for hh, pair, _ in heads]
    g_t = [(g_i[rows[i]] - bcum[rows[i]]).T for i in range(nchunk)]
    log_d, m_inter, m_t = {}, {}, {}
    for i, hh, pair in keys:
        j = 4 * d + hh
        bcol = bcum[rows[i], j:j + 1]
        log_d[i, hh] = jnp.where(mask, bcol + g_t[i][j:j + 1, :], NEG_BIG)
        m_inter[i, hh] = bcol + m_before[i][:, j:j + 1]
    for i, hh, pair in keys:
        m_t[i, hh] = jnp.maximum(m_inter[i, hh], jnp.max(log_d[i, hh], axis=1, keepdims=True))
    pw, s_int, qc, pv = {}, {}, {}, {}
    for i, hh, pair in keys:
        pw[i, hh] = jnp.where(mask, jnp.exp(log_d[i, hh] - m_t[i, hh]), 0.0) * qk[i, hh]
        s_int[i, hh] = jnp.exp(m_inter[i, hh] - m_t[i, hh])
    for i, hh, pair in keys:
        qc[i, hh] = _mm(qh[i, hh], c_before[i, pair])
        pv[i, hh] = _mm(pw[i, hh], v[rows[i], 128 * hh:128 * (hh + 1)])
    den = {}
    for i, hh, pair in keys:
        psl = slice(128 * pair, 128 * (pair + 1))
        den[i, hh] = (s_int[i, hh] * jnp.sum(qh[i, hh] * n_before[i][:, psl], axis=1, keepdims=True)
                      + jnp.sum(pw[i, hh], axis=1, keepdims=True))
    outs = [[None] * ML_H for _ in range(nchunk)]
    for i, hh, pair in keys:
        outs[i][hh] = ((s_int[i, hh] * qc[i, hh] + pv[i, hh])
                       / jnp.maximum(jnp.abs(den[i, hh]), jnp.exp(-m_t[i, hh])))
    return outs, c_cur, n_cur, m


def _mlstm_scan_kernel(fq, fg, bq, bg, bias_ref, tri_ref, hf_ref, hb_ref, c_ref, n_ref, m_ref):
    start = pl.program_id(1) == 0
    npair = ML_H // 2
    prev = [([jnp.where(start, 0.0, c_ref[d, p]) for p in range(npair)],
             jnp.where(start, 0.0, n_ref[d]), jnp.where(start, 0.0, m_ref[d])) for d in range(2)]
    res = [_mlstm_dir(fq, fg, False, 0, bias_ref, tri_ref, *prev[0]),
           _mlstm_dir(bq, bg, True, 1, bias_ref, tri_ref, *prev[1])]
    for d, h_ref in enumerate((hf_ref, hb_ref)):
        outs, c_new, n_new, m_new = res[d]
        for i, heads in enumerate(outs):
            for hh, h in enumerate(heads):
                h_ref[0, CHUNK * i:CHUNK * (i + 1), 128 * hh:128 * (hh + 1)] = h
        for p in range(npair):
            c_ref[d, p] = c_new[p]
        n_ref[d] = n_new
        m_ref[d] = m_new


def _mlstm_scan(p_ml, prm):
    bsz, s, _ = p_ml.shape
    nc = s // SCAN_ROWS
    consts = (prm["ml_bias"], prm["trib"])
    fw = lambda w, cb: pl.BlockSpec((1, SCAN_ROWS, w), lambda b, c: (b, c, cb))
    bw = lambda w, cb: pl.BlockSpec((1, SCAN_ROWS, w), lambda b, c: (b, nc - 1 - c, cb))
    return pl.pallas_call(
        _mlstm_scan_kernel,
        grid=(bsz, nc),
        in_specs=[fw(1024, 0), fw(256, 6), bw(1024, 0), bw(256, 6)] + [_full(x.shape) for x in consts],
        out_specs=[fw(MIX_W, 0), bw(MIX_W, 0)],
        out_shape=[jax.ShapeDtypeStruct((bsz, s, MIX_W), F32)] * 2,
        scratch_shapes=[pltpu.VMEM((2, ML_H // 2, 128, 128), F32), pltpu.VMEM((2, 1, 256), F32),
                        pltpu.VMEM((2, 1, 128), F32)],
        compiler_params=_params(("arbitrary", "arbitrary")),
        name="mlstm_scan",
    )(p_ml, p_ml, p_ml, p_ml, *consts)


def _lru_dir(views, first, last, rev, d, cw_ref, cb_ref, wax_ref, bax_ref, lam_ref, carry):
    m_ref, p_ref, n_ref = views
    x = m_ref[0]
    pv = jnp.where(first, 0.0, p_ref[0, HALO - 1:HALO, :])
    nv = jnp.where(last, 0.0, n_ref[0, 0:2, :])
    cw = cw_ref[...]
    xc = (_shift_rows(x, pv, None, -1) * cw[0:1] + x * cw[1:2] + _shift_rows(x, None, nv, 1) * cw[2:3]
          + _shift_rows(x, None, nv, 2) * cw[3:4] + cb_ref[...])
    rr = _mm(xc, wax_ref[d]) + bax_ref[d]
    log_a = -RG_C * _softplus(-lam_ref[d]) * _sigmoid(rr[:, :MIX_W])
    a = jnp.exp(log_a)
    mult = jnp.sqrt(jnp.maximum(1.0 - jnp.exp(2.0 * log_a), 0.0))
    bv = mult * (_sigmoid(rr[:, MIX_W:]) * xc)
    av = a

    n = x.shape[0]
    ngroups = n // 8
    av, bv = av.reshape(ngroups, 8, MIX_W), bv.reshape(ngroups, 8, MIX_W)
    row8 = _iota((1, 8, 1), 1)
    for sh in (1, 2, 4):
        shift = (8 - sh) if rev else sh
        a_sh, b_sh = pltpu.roll(av, shift, 1), pltpu.roll(bv, shift, 1)
        valid = (row8 < 8 - sh) if rev else (row8 >= sh)
        bv = jnp.where(valid, av * b_sh + bv, bv)
        av = jnp.where(valid, av * a_sh, av)
    av, bv = av.reshape(n, MIX_W), bv.reshape(n, MIX_W)
    entering = [None] * ngroups
    h = carry
    for g in (range(ngroups - 1, -1, -1) if rev else range(ngroups)):
        r = 8 * g if rev else 8 * g + 7
        entering[g] = h
        h = bv[r:r + 1] + av[r:r + 1] * h
    h_in = jnp.concatenate([jnp.broadcast_to(e, (8, MIX_W)) for e in entering], axis=0)
    return bv + av * h_in, h


def _lru_scan_kernel(nc, fm, fp, fn, bm, bp, bn, cw_ref, cb_ref, wax_ref, bax_ref, lam_ref, hf_ref, hb_ref,
                     carry_ref):
    c = pl.program_id(1)
    carry = [jnp.where(c == 0, 0.0, carry_ref[d]) for d in range(2)]
    common = (cw_ref, cb_ref, wax_ref, bax_ref, lam_ref)
    hf, cf = _lru_dir((fm, fp, fn), c == 0, c == nc - 1, False, 0, *common, carry[0])
    hb, cb = _lru_dir((bm, bp, bn), c == nc - 1, c == 0, True, 1, *common, carry[1])
    hf_ref[0] = hf
    hb_ref[0] = hb
    carry_ref[0] = cf
    carry_ref[1] = cb


def _lru_scan(p_lru, prm):
    bsz, s, _ = p_lru.shape
    nc = s // LRU_CHUNK
    consts = (prm["lru_cw"], prm["lru_cb"], prm["lru_wax"], prm["lru_bax"], prm["lru_lam"])
    out = lambda rev: pl.BlockSpec((1, LRU_CHUNK, MIX_W), (lambda b, c: (b, nc - 1 - c, 0)) if rev
                                   else (lambda b, c: (b, c, 0)))
    return pl.pallas_call(
        functools.partial(_lru_scan_kernel, nc),
        grid=(bsz, nc),
        in_specs=_seq_views(MIX_W, LRU_CHUNK, nc, 0, True) + [_full(x.shape) for x in consts],
        out_specs=[out(False), out(True)],
        out_shape=[jax.ShapeDtypeStruct((bsz, s, MIX_W), F32)] * 2,
        scratch_shapes=[pltpu.VMEM((2, 1, MIX_W), F32)],
        compiler_params=_params(("arbitrary", "arbitrary")),
        name="lru_scan",
    )(*([p_lru] * 6), *consts)


HG_LEVELS = (32, 16, 8, 4, 2, 1)


def _hgrn_lower_bound(lb_ref, layer, d):
    x = lb_ref[d]
    e = jnp.exp(x - jnp.max(x, axis=0, keepdims=True))
    sm = e / jnp.sum(e, axis=0, keepdims=True)
    acc = sm[0:1]
    for j in range(1, layer + 1):
        acc = acc + sm[j:j + 1]
    return jnp.clip(acc - sm[0:1], 0.0, 1.0)


def _hgrn_dir(q_ref, f_ref, i_ref, rev, d, layer, lb_ref, tri_ref, sel_ref, states):
    lb = _hgrn_lower_bound(lb_ref, layer, d)
    qraw, fp, v = q_ref[0], f_ref[0], i_ref[0]
    nchunk = qraw.shape[0] // CHUNK
    q = qraw * _sigmoid(qraw)
    logf = jnp.log(lb + (1.0 - lb) * _sigmoid(fp))
    kd = (1.0 - lb) * _sigmoid(-fp)
    gc = _mm_exact_l(tri_ref[d], logf)
    rows = [slice(CHUNK * i, CHUNK * (i + 1)) for i in range(nchunk)]
    tot_rows = [gc[CHUNK * i:CHUNK * i + 1] if rev else gc[CHUNK * (i + 1) - 1:CHUNK * (i + 1)]
                for i in range(nchunk)]
    gtot = jnp.concatenate([jnp.broadcast_to(t, (CHUNK, MIX_W)) for t in tot_rows], axis=0)
    q_in = q * jnp.exp(gc)
    k_out = kd * jnp.exp(gtot - gc)
    gcb = gc.astype(BF16)
    ref_chunks = [jnp.dot(sel_ref[d], gcb[rows[i]], preferred_element_type=F32) for i in range(nchunk)]
    refs = [jnp.concatenate([rc[CHUNK * lv:CHUNK * (lv + 1)] for rc in ref_chunks], axis=0)
            for lv in range(len(HG_LEVELS))]

    row = _iota((CHUNK, CHUNK), 0)
    col = _iota((CHUNK, CHUNK), 1)
    rowc = jnp.bitwise_and(_iota((q.shape[0], 1), 0), CHUNK - 1)
    qs, ks, same = [q], [kd], [row == col]
    for lv, half in enumerate(HG_LEVELS):
        in_hi = jnp.bitwise_and(rowc, 2 * half - 1) >= half
        is_q = jnp.logical_not(in_hi) if rev else in_hi
        qs.append(jnp.where(is_q, q * jnp.exp(gc - refs[lv]), 0.0))
        ks.append(jnp.where(is_q, 0.0, kd * jnp.exp(refs[lv] - gc)))
        sh = int(math.log2(2 * half))
        same.append(jnp.right_shift(row, sh) == jnp.right_shift(col, sh))

    parts, att, upd = {}, {}, {}
    for i in range(nchunk):
        for hh in range(HG_H):
            sl = slice(128 * hh, 128 * (hh + 1))
            parts[i, hh] = [_mm_nt(ql[rows[i], sl], kl[rows[i], sl]) for ql, kl in zip(qs, ks)]
            upd[i, hh] = _mm_tn(v[rows[i], sl], k_out[rows[i], sl])
    for key, plist in parts.items():
        acc = None
        for part, msk in zip(plist, same):
            part = jnp.where(msk, part, 0.0)
            acc = part if acc is None else acc + part
        att[key] = acc
    cur = list(states)
    before = {}
    for i in (range(nchunk - 1, -1, -1) if rev else range(nchunk)):
        g_dec = jnp.exp(tot_rows[i])
        for hh in range(HG_H):
            before[i, hh] = cur[hh]
            cur[hh] = cur[hh] * g_dec[:, 128 * hh:128 * (hh + 1)] + upd[i, hh]
    outs = [[None] * HG_H for _ in range(nchunk)]
    for i in range(nchunk):
        for hh in range(HG_H):
            sl = slice(128 * hh, 128 * (hh + 1))
            outs[i][hh] = _mm(att[i, hh], v[rows[i], sl]) + _mm_nt(q_in[rows[i], sl], before[i, hh])
    return outs, cur


def _hgrn_scan_kernel(layer, fq, ff, fi, bq, bf, bi, lb_ref, tri_ref, sel_ref, of_ref, ob_ref, s_ref):
    start = pl.program_id(1) == 0
    prev = [[jnp.where(start, 0.0, s_ref[d, hh]) for hh in range(HG_H)] for d in range(2)]
    res = [_hgrn_dir(fq, ff, fi, False, 0, layer, lb_ref, tri_ref, sel_ref, prev[0]),
           _hgrn_dir(bq, bf, bi, True, 1, layer, lb_ref, tri_ref, sel_ref, prev[1])]
    for d, o_ref in enumerate((of_ref, ob_ref)):
        outs, new_states = res[d]
        for i, heads in enumerate(outs):
            for hh, o in enumerate(heads):
                o_ref[0, CHUNK * i:CHUNK * (i + 1), 128 * hh:128 * (hh + 1)] = o
        for hh in range(HG_H):
            s_ref[d, hh] = new_states[hh]


def _hgrn_scan(p_hg, hg_lb, layer, prm):
    bsz, s, _ = p_hg.shape
    nc = s // SCAN_ROWS
    consts = (hg_lb, prm["trib"], prm["hg_sel"])
    fw = lambda cb: pl.BlockSpec((1, SCAN_ROWS, MIX_W), lambda b, c: (b, c, cb))
    bw = lambda cb: pl.BlockSpec((1, SCAN_ROWS, MIX_W), lambda b, c: (b, nc - 1 - c, cb))
    return pl.pallas_call(
        functools.partial(_hgrn_scan_kernel, layer),
        grid=(bsz, nc),
        in_specs=[fw(0), fw(1), fw(3), bw(0), bw(2), bw(3)] + [_full(x.shape) for x in consts],
        out_specs=[fw(0), bw(0)],
        out_shape=[jax.ShapeDtypeStruct((bsz, s, MIX_W), F32)] * 2,
        scratch_shapes=[pltpu.VMEM((2, HG_H, HG_D, HG_D), F32)],
        compiler_params=_params(("arbitrary", "arbitrary")),
        name="hgrn_scan",
    )(*([p_hg] * 6), *consts)


def _head_rms128(x, g):
    parts = []
    for hh in range(MIX_W // 128):
        xh = x[:, 128 * hh:128 * (hh + 1)]
        parts.append(xh * lax.rsqrt(jnp.mean(xh * xh, axis=-1, keepdims=True) + 1e-6))
    return jnp.concatenate(parts, axis=1) * g


def _gelu_tanh(x):
    cdf = 0.5 * (1.0 + jnp.tanh(math.sqrt(2.0 / math.pi) * (x + 0.044715 * (x * x * x))))
    return x * cdf


def _merge_kernel(u_ref, h_ref, ya_ref, mlf, mlb, mlo, lrf, lrb, lrg, hgf, hgb, hgg, gt_ref, mln_ref,
                  hgn_ref, wg_ref, wb_ref, wo_ref, o_ref):
    gates = _sigmoid(jnp.dot(u_ref[0], wg_ref[...], preferred_element_type=F32))
    y_b = _sigmoid(mlo[0]) * _head_rms128(mlf[0] + mlb[0], mln_ref[...])
    y_c = (lrf[0] + lrb[0]) * _gelu_tanh(lrg[0])
    gg = hgg[0]
    y_d = (gg * _sigmoid(gg)) * _head_rms128(hgf[0] + hgb[0], hgn_ref[...])
    merged = None
    for n, y in enumerate((ya_ref[0], y_b, y_c, y_d)):
        term = gates[:, D_MODEL * n:D_MODEL * (n + 1)] * _mm(y, wb_ref[n])
        merged = term if merged is None else merged + term
    o_ref[0] = h_ref[0] + gt_ref[0, 0] * _mm(merged, wo_ref[...])


def _merge(u, h, y_a, ml, p_ml, lru, p_lru, hg, p_hg, mod_l, prm):
    bsz, s, _ = h.shape
    row = lambda w, cb=0: pl.BlockSpec((1, ROW_TILE, w), lambda b, i: (b, i, cb))
    consts = (prm["ml_norm"], prm["hg_norm"], prm["w_gate"], prm["w_branch"], prm["w_out"])
    return pl.pallas_call(
        _merge_kernel,
        grid=(bsz, s // ROW_TILE),
        in_specs=[row(D_MODEL), row(D_MODEL), row(MIX_W),
                  row(MIX_W), row(MIX_W), row(MIX_W, 2),
                  row(MIX_W), row(MIX_W), row(MIX_W, 1),
                  row(MIX_W), row(MIX_W), row(MIX_W, 4),
                  _mod_spec(2)] + [_full(x.shape) for x in consts],
        out_specs=row(D_MODEL),
        out_shape=jax.ShapeDtypeStruct((bsz, s, D_MODEL), F32),
        compiler_params=_params(("arbitrary", "arbitrary")),
        name="merge",
    )(u, h, y_a, ml[0], ml[1], p_ml, lru[0], lru[1], p_lru, hg[0], hg[1], p_hg, mod_l, *consts)


def _ffn_kernel(nt, final, hm, hp, hn, g_ref, sc_ref, sh_ref, gt_ref, wup_ref, cw_ref, cb_ref, wdn_ref,
                fg_ref, o_ref):
    i = pl.program_id(1)
    hx = jnp.concatenate([hp[0], hm[0], hn[0]], axis=0)
    n = hx.shape[0]
    u2 = (_rmsnorm_rows(hx, g_ref[...]) * (1.0 + sc_ref[0, 0]) + sh_ref[0, 0]).astype(BF16)
    row = _iota((n, 1), 0)
    valid = jnp.logical_and(jnp.logical_or(row >= HALO, i > 0),
                            jnp.logical_or(row < n - HALO, i < nt - 1))
    cw = cw_ref[...]
    cb = cb_ref[...]
    fc = D_FF // FF_SPLIT
    acc = None
    for j in range(FF_SPLIT):
        zs = []
        for base in (0, D_FF):
            sl = slice(base + fc * j, base + fc * (j + 1))
            y = jnp.where(valid, jnp.dot(u2, wup_ref[:, sl], preferred_element_type=F32), 0.0)
            z = (pltpu.roll(y, 1, 0) * cw[0:1, sl] + y * cw[1:2, sl] + pltpu.roll(y, n - 1, 0) * cw[2:3, sl]
                 + cb[:, sl])
            zs.append(z[HALO:n - HALO])
        val, gate = zs
        act = val * (gate * _sigmoid(gate))
        part = _mm(act, wdn_ref[fc * j:fc * (j + 1), :])
        acc = part if acc is None else acc + part
    out = hm[0] + gt_ref[0, 0] * acc
    if final:
        out = _rmsnorm_rows(out, fg_ref[...])
    o_ref[0] = out


def _conv_ffn(h, mod_l, prm, final_g, final):
    bsz, s, _ = h.shape
    nt = s // ROW_TILE
    consts = (prm["ffn_up"], prm["ffn_cw"], prm["ffn_cb"], prm["ffn_down"], final_g)
    row = pl.BlockSpec((1, ROW_TILE, D_MODEL), lambda b, i: (b, i, 0))
    return pl.pallas_call(
        functools.partial(_ffn_kernel, nt, final),
        grid=(bsz, nt),
        in_specs=_seq_views(D_MODEL, ROW_TILE, nt, 0, True)[:3]
        + [_full((1, D_MODEL)), _mod_spec(4), _mod_spec(3), _mod_spec(5)] + [_full(x.shape) for x in consts],
        out_specs=row,
        out_shape=jax.ShapeDtypeStruct((bsz, s, D_MODEL), F32),
        compiler_params=_params(("arbitrary", "arbitrary")),
        name="conv_ffn",
    )(h, h, h, prm["norm2_g"], mod_l, mod_l, mod_l, *consts)


def _block_diag(blocks):
    g, n, m = blocks.shape
    eye = jnp.eye(g, dtype=blocks.dtype)
    return (eye[:, None, :, None] * blocks[:, :, None, :]).reshape(g * n, g * m)


def _constants():
    idx = np.arange(CHUNK)
    tri = np.stack([idx[:, None] >= idx[None, :], idx[:, None] <= idx[None, :]]).astype(np.float32)
    sel = np.zeros((2, len(HG_LEVELS) * CHUNK, CHUNK), np.float32)
    for lv, half in enumerate(HG_LEVELS):
        start = (idx // (2 * half)) * (2 * half)
        sel[0, CHUNK * lv + idx, start + half - 1] = 1.0
        sel[1, CHUNK * lv + idx, start + half] = 1.0
    lane = np.arange(MIX_W)
    seg = (lane[:, None] // RW_N == lane[None, :] // RW_N).astype(np.float32)
    ridx = np.arange(SCAN_ROWS)
    same_chunk = ridx[:, None] // CHUNK == ridx[None, :] // CHUNK
    trib = np.stack([same_chunk & (ridx[:, None] >= ridx[None, :]),
                     same_chunk & (ridx[:, None] <= ridx[None, :])]).astype(np.float32)
    return (jnp.asarray(tri, BF16), jnp.asarray(sel, BF16), jnp.asarray(seg, BF16), jnp.asarray(trib, BF16))


def _layer_params(l, consts, w_in, rw_mu, rw_w0, rw_w2, rw_a0, rw_a2, rw_g2, rw_kk, rw_ka, rw_rk,
                  rw_lnw, rw_lnb, ml_ibias, ml_fbias, ml_norm, lru_conv_w, lru_conv_b, lru_wa, lru_ba,
                  lru_wx, lru_bx, lru_lam, hg_norm, w_branch, w_out, norm2_g, ffn_up, ffn_conv_w,
                  ffn_conv_b, ffn_down):
    row = lambda x: x.reshape(1, -1)
    w = w_in[l]
    o_ml = RW_COLS
    o_lru = o_ml + 1552
    o_hg = o_lru + 2 * MIX_W
    o_gate = o_hg + 5 * MIX_W
    zeros = jnp.zeros((D_MODEL, 120), F32)
    w_ml = jnp.concatenate([w[:, o_ml:o_ml + 1536], w[:, o_ml + 1536:o_ml + 1544], zeros,
                            w[:, o_ml + 1544:o_ml + 1552], zeros], axis=1)
    z64 = jnp.zeros((64, MIX_W), F32)
    wlr = jnp.stack([jnp.concatenate([jnp.concatenate([rw_w2[l, d], z64], axis=1),
                                      jnp.concatenate([z64, rw_a2[l, d]], axis=1)], axis=0) for d in range(2)])
    pad = jnp.zeros((120,), F32)
    ml_bias = jnp.concatenate([ml_ibias[l].reshape(-1), pad, ml_fbias[l].reshape(-1), pad]).reshape(1, 256)
    wax = jnp.stack([jnp.concatenate([_block_diag(lru_wa[l, d]), _block_diag(lru_wx[l, d])], axis=1)
                     for d in range(2)])
    return {
        "tri": consts[0], "hg_sel": consts[1], "seg64": consts[2], "trib": consts[3],
        "w_rw": w[:, :RW_COLS].astype(BF16), "w_ml": w_ml.astype(BF16),
        "w_lru": w[:, o_lru:o_hg].astype(BF16), "w_hg": w[:, o_hg:o_gate].astype(BF16),
        "w_gate": w[:, o_gate:].astype(BF16),
        "mu": row(rw_mu[l]),
        "w0a0": jnp.concatenate([rw_w0[l], rw_a0[l]], axis=1).reshape(2, 1, 2 * MIX_W),
        "wlr": wlr.astype(BF16), "kk": row(rw_kk[l]), "ka": row(rw_ka[l]), "rk": row(rw_rk[l]),
        "lnw": row(rw_lnw[l]), "lnb": row(rw_lnb[l]), "g2": rw_g2[l].astype(BF16),
        "ml_bias": ml_bias, "ml_norm": row(ml_norm[l]),
        "lru_cw": lru_conv_w[l], "lru_cb": row(lru_conv_b[l]), "lru_wax": wax.astype(BF16),
        "lru_bax": jnp.concatenate([lru_ba[l], lru_bx[l]], axis=1).reshape(2, 1, 2 * MIX_W),
        "lru_lam": lru_lam[l].reshape(2, 1, MIX_W),
        "hg_norm": row(hg_norm[l]),
        "w_branch": w_branch[l].astype(BF16), "w_out": w_out[l].astype(BF16),
        "norm2_g": row(norm2_g[l]), "ffn_up": ffn_up[l].astype(BF16), "ffn_cw": ffn_conv_w[l],
        "ffn_cb": row(ffn_conv_b[l]), "ffn_down": ffn_down[l].astype(BF16),
    }


def kernel(x, c, ada_w, ada_b, norm1_g, w_in, rw_mu, rw_w0, rw_w2, rw_a0, rw_a2, rw_g2, rw_kk, rw_ka, rw_rk, rw_lnw, rw_lnb, ml_ibias, ml_fbias, ml_norm, lru_conv_w, lru_conv_b, lru_wa, lru_ba, lru_wx, lru_bx, lru_lam, hg_lb, hg_norm, w_branch, w_out, norm2_g, ffn_up, ffn_conv_w, ffn_conv_b, ffn_down, final_g):
    depth = w_in.shape[0]
    assert x.shape[1] % LRU_CHUNK == 0 and x.shape[2] == D_MODEL
    consts = _constants()
    mod = _ada_mod(c.astype(F32), ada_w, ada_b)
    h = x.astype(F32)
    final_row = final_g.reshape(1, D_MODEL)
    hg_lb = jnp.swapaxes(hg_lb.astype(F32), 0, 1)
    for l in range(depth):
        prm = _layer_params(l, consts, w_in, rw_mu, rw_w0, rw_w2, rw_a0, rw_a2, rw_g2, rw_kk, rw_ka,
                            rw_rk, rw_lnw, rw_lnb, ml_ibias, ml_fbias, ml_norm, lru_conv_w, lru_conv_b,
                            lru_wa, lru_ba, lru_wx, lru_bx, lru_lam, hg_norm, w_branch, w_out, norm2_g,
                            ffn_up, ffn_conv_w, ffn_conv_b, ffn_down)
        mod_l = mod[l]
        u, p_rw, p_ml, p_lru, p_hg = _project(h, mod_l, norm1_g[l].reshape(1, D_MODEL), prm["w_rw"],
                                              prm["w_ml"], prm["w_lru"], prm["w_hg"])
        y_a = _rwkv_finalize(p_rw, *_rwkv_scan(p_rw, prm), prm)
        ml = _mlstm_scan(p_ml, prm)
        lru = _lru_scan(p_lru, prm)
        hg = _hgrn_scan(p_hg, hg_lb, l, prm)
        h = _merge(u, h, y_a, ml, p_ml, lru, p_lru, hg, p_hg, mod_l, prm)
        h = _conv_ffn(h, mod_l, prm, final_row, l == depth - 1)
    return h
```

```python
import functools
import math

import numpy as np
import jax
import jax.numpy as jnp
from jax import lax
from jax.experimental import pallas as pl
from jax.experimental.pallas import tpu as pltpu

F32 = jnp.float32
BF16 = jnp.bfloat16

D_MODEL = 1024
MIX_W = 512
N_BRANCH = 4
RW_N = 64
RW_H = 8
RW_GN_EPS = 64e-5
RW_COLS = 1792
ML_H = 4
ML_DV = 128
ML_DK = 64
ML_COLS_PAD = 1792
LRU_CONV = 4
RG_C = 8.0
HG_H = 4
HG_D = 128
D_FF = 2816
NEG_BIG = -1e30

CHUNK = 64
SCAN_ROWS = 256
LRU_CHUNK = 256
ROW_TILE = 256
FFN_TILE = 512
HALO = 8
FF_SPLIT = 2
VMEM_LIMIT = 56 * 1024 * 1024


def _mm(a, b):
    return jnp.dot(a.astype(BF16), b.astype(BF16), preferred_element_type=F32)


def _mm_nt(a, b):
    return lax.dot_general(a.astype(BF16), b.astype(BF16), (((1,), (1,)), ((), ())),
                           preferred_element_type=F32)


def _mm_tn(a, b):
    return lax.dot_general(a.astype(BF16), b.astype(BF16), (((0,), (0,)), ((), ())),
                           preferred_element_type=F32)


def _split3(x):
    hi = x.astype(BF16)
    r = x - hi.astype(F32)
    mid = r.astype(BF16)
    lo = (r - mid.astype(F32)).astype(BF16)
    return hi, mid, lo


def _mm_exact_l(sel, x):
    n = x.shape[1]
    r = jnp.dot(sel, jnp.concatenate(_split3(x), axis=1), preferred_element_type=F32)
    return r[:, :n] + r[:, n:2 * n] + r[:, 2 * n:]


def _sigmoid(x):
    return jax.nn.sigmoid(x)


def _softplus(x):
    return jnp.maximum(x, 0.0) + jnp.log(1.0 + jnp.exp(-jnp.abs(x)))


def _rmsnorm_rows(x, g, eps=1e-6):
    return x * lax.rsqrt(jnp.mean(x * x, axis=-1, keepdims=True) + eps) * g


def _iota(shape, dim):
    return lax.broadcasted_iota(jnp.int32, shape, dim)


def _shift_rows(x, prev_row, next_row, k):
    n = x.shape[0]
    row = _iota((n, 1), 0)
    rolled = pltpu.roll(x, (-k) % n, 0)
    if k == -1:
        return jnp.where(row == 0, prev_row, rolled)
    out = rolled
    for j in range(k):
        out = jnp.where(row == n - k + j, next_row[j:j + 1], out)
    return out


def _params(sem):
    return pltpu.CompilerParams(dimension_semantics=sem, vmem_limit_bytes=VMEM_LIMIT)


def _full(shape):
    nd = len(shape)
    return pl.BlockSpec(shape, lambda *_: (0,) * nd)


def _seq_views(width, rows, nsteps, col_block, halo):
    per = rows // HALO
    last = nsteps * per - 1
    specs = []
    for rev in (False, True):
        idx = (lambda c: nsteps - 1 - c) if rev else (lambda c: c)
        specs.append(pl.BlockSpec((1, rows, width), lambda b, c, idx=idx: (b, idx(c), col_block)))
        if halo:
            specs.append(pl.BlockSpec(
                (1, HALO, width), lambda b, c, idx=idx: (b, jnp.maximum(idx(c) * per - 1, 0), col_block)))
            specs.append(pl.BlockSpec(
                (1, HALO, width), lambda b, c, idx=idx: (b, jnp.minimum((idx(c) + 1) * per, last), col_block)))
    return specs


def _tri_masks(rev):
    row = _iota((CHUNK, 128), 0)
    col = jnp.bitwise_and(_iota((CHUNK, 128), 1), CHUNK - 1)
    if rev:
        strict, incl = col > row, col >= row
    else:
        strict, incl = col < row, col <= row
    blk = jnp.right_shift(col, 4) == jnp.right_shift(row, 4)
    return strict, incl, blk, col == row


def _ada_kernel(c_ref, w_ref, b_ref, o_ref):
    c = c_ref[...]
    cond = c * _sigmoid(c)
    o_ref[0, 0] = jnp.dot(cond, w_ref[0], preferred_element_type=F32,
                          precision=lax.Precision.HIGHEST) + b_ref[0, 0]


def _ada_mod(c, ada_w, ada_b):
    depth = ada_w.shape[0]
    bsz = c.shape[0]
    out = pl.pallas_call(
        _ada_kernel,
        grid=(depth, 6),
        in_specs=[_full((bsz, D_MODEL)),
                  pl.BlockSpec((1, D_MODEL, D_MODEL), lambda l, j: (l, 0, j)),
                  pl.BlockSpec((1, 1, 1, D_MODEL), lambda l, j: (l, j, 0, 0))],
        out_specs=pl.BlockSpec((1, 1, bsz, D_MODEL), lambda l, j: (l, j, 0, 0)),
        out_shape=jax.ShapeDtypeStruct((depth, 6, bsz, D_MODEL), F32),
        compiler_params=_params(("arbitrary", "arbitrary")),
        name="ada_mod",
    )(c, ada_w, ada_b.reshape(depth, 6, 1, D_MODEL))
    return out.reshape(depth, 6, bsz, 1, D_MODEL)


def _mod_spec(idx):
    return pl.BlockSpec((1, 1, 1, D_MODEL), lambda b, i: (idx, b, 0, 0))


def _proj_kernel(h_ref, g_ref, sc_ref, sh_ref, wrw, wml, wlru, whg, u_ref, prw, pml, plru, phg):
    u = _rmsnorm_rows(h_ref[0], g_ref[...]) * (1.0 + sc_ref[0, 0]) + sh_ref[0, 0]
    ub = u.astype(BF16)
    u_ref[0] = ub
    for w_ref, o_ref in ((wrw, prw), (wml, pml), (wlru, plru), (whg, phg)):
        o_ref[0] = jnp.dot(ub, w_ref[...], preferred_element_type=F32)


def _project(h, mod_l, g1, wrw, wml, wlru, whg):
    bsz, s, _ = h.shape
    widths = [w.shape[1] for w in (wrw, wml, wlru, whg)]
    row = lambda w: pl.BlockSpec((1, ROW_TILE, w), lambda b, i: (b, i, 0))
    return pl.pallas_call(
        _proj_kernel,
        grid=(bsz, s // ROW_TILE),
        in_specs=[row(D_MODEL), _full((1, D_MODEL)), _mod_spec(1), _mod_spec(0)]
        + [_full(w.shape) for w in (wrw, wml, wlru, whg)],
        out_specs=[row(D_MODEL)] + [row(w) for w in widths],
        out_shape=[jax.ShapeDtypeStruct((bsz, s, D_MODEL), BF16)]
        + [jax.ShapeDtypeStruct((bsz, s, w), F32) for w in widths],
        compiler_params=_params(("arbitrary", "arbitrary")),
        name="in_proj",
    )(h, g1, mod_l, mod_l, wrw, wml, wlru, whg)


def _rwkv_shifted(m_ref, p_ref, n_ref, first, last, mu):
    x = m_ref[0]
    pv = jnp.where(first, 0.0, p_ref[0, HALO - 1:HALO, :])
    nv = jnp.where(last, 0.0, n_ref[0, 0:1, :])
    prev = _shift_rows(x, pv, None, -1)
    nxt = _shift_rows(x, None, nv, 1)
    return x + mu * (0.5 * (prev + nxt) - x)


def _rwkv_lowrank(xs, d, w0a0_ref, wlr_ref):
    blk = xs[:, 1536:1664]
    lane = _iota(blk.shape, 1)
    lrin = jnp.where(lane < 64, jnp.tanh(blk), blk)
    return _mm(lrin, wlr_ref[d]) + w0a0_ref[d]


def _rwkv_dir(views, first, last, rev, d, mu_ref, w0a0_ref, wlr_ref, kk_ref, ka_ref, seg_ref, tri_ref,
              lanes_ref):
    xs = _rwkv_shifted(*views, first, last, mu_ref[...])
    r, k, v = xs[:, 0:512], xs[:, 512:1024], xs[:, 1024:1536]
    lr = _rwkv_lowrank(xs, d, w0a0_ref, wlr_ref)
    lw = -jnp.exp(-_softplus(-lr[:, :MIX_W]) - 0.5)
    a = _sigmoid(lr[:, MIX_W:])
    kkm = k * kk_ref[...]
    ss = _mm(kkm * kkm, seg_ref[...])
    kkn = kkm * lax.rsqrt(jnp.maximum(ss, 1e-24))
    a_s, b_s = -kkn, kkn * a
    k_s = k * (1.0 + (a - 1.0) * ka_ref[...])

    cum = _mm_exact_l(tri_ref[d], lw)
    nchunk = xs.shape[0] // CHUNK
    tot_rows = [cum[CHUNK * i:CHUNK * i + 1] if rev else cum[CHUNK * (i + 1) - 1:CHUNK * (i + 1)]
                for i in range(nchunk)]
    ctot = jnp.concatenate([jnp.broadcast_to(t, (CHUNK, MIX_W)) for t in tot_rows], axis=0)
    e_neg = jnp.exp(-cum)
    e_hat = jnp.exp(ctot - cum)
    at, rt = a_s * jnp.exp(cum - lw), r * jnp.exp(cum)
    bt, kt = b_s * e_neg, k_s * e_neg
    bh, kh = b_s * e_hat, k_s * e_hat

    strict, incl, blk16, eye = _tri_masks(rev)
    brow = _iota((128, 128), 0) < 64
    bcol = _iota((128, 128), 1) < 64
    bdmask = brow == bcol
    masks = (strict, incl, blk16, eye, bdmask, lanes_ref[0], lanes_ref[1])
    chains = []
    for i in range(nchunk):
        rs = slice(CHUNK * i, CHUNK * (i + 1))
        gam = jnp.exp(tot_rows[i])
        chains.append([_rwkv_pair_free(at[rs, sl], rt[rs, sl], bt[rs, sl], kt[rs, sl], bh[rs, sl], kh[rs, sl],
                                       v[rs, sl], gam[:, sl], masks)
                       for sl in (slice(128 * j, 128 * (j + 1)) for j in range(RW_H // 2))])
    return chains


def _rwkv_pair_free(at, rt, bt, kt, bh, kh, vj, gam, masks):
    strict, incl, blk16, eye, bdmask, lo, hi = masks
    b16 = lambda x: x.astype(BF16)
    bdr = lambda xb: jnp.concatenate([xb * lo, xb * hi], axis=0)
    dot = lambda l, r: jnp.dot(l, r, preferred_element_type=F32)
    ar = b16(jnp.concatenate([at, rt], axis=0))
    bk = jnp.concatenate([bdr(b16(bt)), bdr(b16(kt))], axis=0)
    sc = _mm_nt(ar, bk)
    yield
    a_ab = jnp.where(strict, sc[:CHUNK, :128], 0.0)
    a_ak = jnp.where(strict, sc[:CHUNK, 128:], 0.0)
    r_b = jnp.where(incl, sc[CHUNK:, :128], 0.0)
    r_k = jnp.where(incl, sc[CHUNK:, 128:], 0.0)
    rows2 = lambda x, y: b16(jnp.concatenate([x, y], axis=0))
    a_d = jnp.where(blk16, a_ab, 0.0)
    a_o = a_ab - a_d
    a_d16, vbd = b16(a_d), bdr(b16(vj))
    a2 = dot(a_d16, bdr(a_d16))
    akrk = dot(rows2(a_ak, r_k), vbd)
    yield
    t = eye.astype(F32) + a_d
    prod = dot(rows2(t, a2), bdr(b16(a2)))
    t, a4 = t + prod[:CHUNK], prod[CHUNK:]
    yield
    prod = dot(rows2(t, a4), bdr(b16(a4)))
    t, a8 = t + prod[:CHUNK], prod[CHUNK:]
    yield
    t = t + dot(b16(t), bdr(b16(a8)))
    yield
    m = dot(b16(a_o), bdr(b16(t)))
    yield
    prod = dot(rows2(t, m), bdr(b16(m)))
    t, m2 = t + prod[:CHUNK], prod[CHUNK:]
    yield
    t = t + dot(b16(t), bdr(b16(m2)))
    yield
    return dict(ar=ar, t=b16(t), avk=akrk[:CHUNK], rkv=akrk[CHUNK:], rb=b16(r_b), vj=b16(vj),
                bhk=b16(jnp.concatenate([bh, kh], axis=0)), gam=gam, bdmask=bdmask, lo=lo, hi=hi)


def _rwkv_pair_carry(free, sbd):
    lo, hi = free["lo"], free["hi"]
    bdr = lambda xb: jnp.concatenate([xb * lo, xb * hi], axis=0)
    ars = _mm_nt(free["ar"], sbd)
    yield
    u = jnp.dot(free["t"], bdr((ars[:CHUNK] + free["avk"]).astype(BF16)), preferred_element_type=F32)
    yield
    u16 = u.astype(BF16)
    y = ars[CHUNK:] + free["rkv"] + jnp.dot(free["rb"], bdr(u16), preferred_element_type=F32)
    upd = _mm_tn(jnp.concatenate([u16, free["vj"]], axis=0), free["bhk"])
    return y, free["gam"] * sbd + jnp.where(free["bdmask"], upd, 0.0)


def _run_round_robin(chains):
    results = [None] * len(chains)
    active = list(range(len(chains)))
    while active:
        for i in list(active):
            try:
                next(chains[i])
            except StopIteration as stop:
                results[i] = stop.value
                active.remove(i)
    return results


def _rwkv_scan_kernel(nc, fm, fp, fn, bm, bp, bn, mu_ref, w0a0_ref, wlr_ref, kk_ref, ka_ref, seg_ref,
                      tri_ref, lanes_ref, yf_ref, yb_ref, sf_ref, sb_ref):
    c = pl.program_id(1)
    npair = RW_H // 2
    nchunk = SCAN_ROWS // CHUNK
    states = [[jnp.where(c == 0, 0.0, s_ref[j]) for j in range(npair)] for s_ref in (sf_ref, sb_ref)]
    common = (mu_ref, w0a0_ref, wlr_ref, kk_ref, ka_ref, seg_ref, tri_ref, lanes_ref)
    free_f = _rwkv_dir((fm, fp, fn), c == 0, c == nc - 1, False, 0, *common)
    free_b = _rwkv_dir((bm, bp, bn), c == nc - 1, c == 0, True, 1, *common)
    res = _run_round_robin([g for chunk in free_f + free_b for g in chunk])
    free = [res[:nchunk * npair], res[nchunk * npair:]]
    y_refs = (yf_ref, yb_ref)
    outs = []
    for step in range(nchunk):
        chunk_of = (step, nchunk - 1 - step)
        res = _run_round_robin([_rwkv_pair_carry(free[dd][chunk_of[dd] * npair + j], states[dd][j])
                                for dd in range(2) for j in range(npair)])
        for dd in range(2):
            for j in range(npair):
                y, states[dd][j] = res[dd * npair + j]
                outs.append((dd, chunk_of[dd], j, y))
    for dd, i, j, y in outs:
        y_refs[dd][0, CHUNK * i:CHUNK * (i + 1), 128 * j:128 * (j + 1)] = y
    for dd, s_ref in enumerate((sf_ref, sb_ref)):
        for j in range(npair):
            s_ref[j] = states[dd][j]


def _rwkv_scan(p_rw, prm):
    bsz, s, _ = p_rw.shape
    nc = s // SCAN_ROWS
    consts = (prm["mu"], prm["w0a0"], prm["wlr"], prm["kk"], prm["ka"], prm["seg64"], prm["trib"],
              prm["lanes"])
    out = lambda rev: pl.BlockSpec((1, SCAN_ROWS, MIX_W), (lambda b, c: (b, nc - 1 - c, 0)) if rev
                                   else (lambda b, c: (b, c, 0)))
    return pl.pallas_call(
        functools.partial(_rwkv_scan_kernel, nc),
        grid=(bsz, nc),
        in_specs=_seq_views(RW_COLS, SCAN_ROWS, nc, 0, True) + [_full(x.shape) for x in consts],
        out_specs=[out(False), out(True)],
        out_shape=[jax.ShapeDtypeStruct((bsz, s, MIX_W), F32)] * 2,
        scratch_shapes=[pltpu.VMEM((RW_H // 2, 128, 128), F32)] * 2,
        compiler_params=_params(("arbitrary", "arbitrary")),
        name="rwkv_scan",
    )(*([p_rw] * 6), *consts)


def _rwkv_fin_kernel(nt, pm, pp, pn, yf_ref, yb_ref, mu_ref, w0a0_ref, wlr_ref, ka_ref, rk_ref, lnw_ref,
                     lnb_ref, g2_ref, seg_ref, o_ref):
    i = pl.program_id(1)
    xs = _rwkv_shifted(pm, pp, pn, i == 0, i == nt - 1, mu_ref[...])
    r, k, v = xs[:, 0:512], xs[:, 512:1024], xs[:, 1024:1536]
    kts = []
    for d in range(2):
        a = _sigmoid(_rwkv_lowrank(xs, d, w0a0_ref, wlr_ref)[:, MIX_W:])
        kts.append(k * (1.0 + (a - 1.0) * ka_ref[...]))
    y = yf_ref[0] + yb_ref[0]
    inv_n = 1.0 / RW_N
    mean = _mm(y, seg_ref[...]) * inv_n
    yc = y - mean
    var = _mm(yc * yc, seg_ref[...]) * inv_n
    y = yc * lax.rsqrt(var + RW_GN_EPS) * lnw_ref[...] + lnb_ref[...]
    k_bonus = 0.5 * (kts[0] + kts[1])
    bonus = _mm(r * k_bonus * rk_ref[...], seg_ref[...]) * v
    g = _mm(_sigmoid(xs[:, 1664:1792]), g2_ref[...])
    o_ref[0] = ((y + bonus) * g).astype(BF16)


def _rwkv_finalize(p_rw, yf, yb, prm):
    bsz, s, _ = p_rw.shape
    nt = s // ROW_TILE
    consts = (prm["mu"], prm["w0a0"], prm["wlr"], prm["ka"], prm["rk"], prm["lnw"], prm["lnb"], prm["g2"],
              prm["seg64"])
    row = pl.BlockSpec((1, ROW_TILE, MIX_W), lambda b, i: (b, i, 0))
    return pl.pallas_call(
        functools.partial(_rwkv_fin_kernel, nt),
        grid=(bsz, nt),
        in_specs=_seq_views(RW_COLS, ROW_TILE, nt, 0, True)[:3] + [row, row] + [_full(x.shape) for x in consts],
        out_specs=row,
        out_shape=jax.ShapeDtypeStruct((bsz, s, MIX_W), BF16),
        compiler_params=_params(("arbitrary", "arbitrary")),
        name="rwkv_finalize",
    )(p_rw, p_rw, p_rw, yf, yb, *consts)


def _mlstm_dir(qkv_ref, gate_ref, rev, d, bias_ref, tri_ref, c_prev, n_prev, m_prev):
    qkv = qkv_ref[0]
    nchunk = qkv.shape[0] // CHUNK
    q = qkv[:, 0:256] * (ML_DK ** -0.5)
    k = qkv[:, 256:512]
    v = qkv[:, 512:1024]
    gates = gate_ref[0] + bias_ref[...]
    g_i = gates[:, 0:128]
    g_f = -_softplus(-gates[:, 128:256])
    bcum = _mm_exact_l(tri_ref[d], g_f)
    order = range(nchunk - 1, -1, -1) if rev else range(nchunk)
    rows = [slice(CHUNK * i, CHUNK * (i + 1)) for i in range(nchunk)]

    m = m_prev
    m_before, w_c, dec = [None] * nchunk, [None] * nchunk, [None] * nchunk
    for i in order:
        g_end = bcum[CHUNK * i:CHUNK * i + 1] if rev else bcum[CHUNK * (i + 1) - 1:CHUNK * (i + 1)]
        a_end = g_end - bcum[rows[i]] + g_i[rows[i]]
        m_new = jnp.maximum(g_end + m, jnp.max(a_end, axis=0, keepdims=True))
        w_c[i] = jnp.exp(a_end - m_new)
        dec[i] = jnp.exp(g_end + m - m_new)
        m_before[i] = m
        m = m_new

    row = _iota((CHUNK, CHUNK), 0)
    col = _iota((CHUNK, CHUNK), 1)
    mask = (col >= row) if rev else (col <= row)
    lane_lo = _iota((CHUNK, 128), 1) < 64
    heads = [(hh,) + divmod(hh, 2) for hh in range(ML_H)]

    qh, qk, contrib, kw = {}, {}, {}, {}
    for i in range(nchunk):
        for hh, pair, half in heads:
            psl = slice(128 * pair, 128 * (pair + 1))
            j = 4 * d + hh
            hm = lane_lo if half == 0 else jnp.logical_not(lane_lo)
            kp = k[rows[i], psl]
            qh[i, hh] = jnp.where(hm, q[rows[i], psl], 0.0)
            kw[i, hh] = jnp.where(hm, kp, 0.0) * w_c[i][:, j:j + 1]
    for i in range(nchunk):
        for hh, pair, half in heads:
            qk[i, hh] = _mm_nt(qh[i, hh], k[rows[i], 128 * pair:128 * (pair + 1)])
            contrib[i, hh] = _mm_tn(kw[i, hh], v[rows[i], 128 * hh:128 * (hh + 1)])

    c_cur, n_cur = list(c_prev), n_prev
    c_before, n_before = {}, {}
    for i in order:
        n_before[i] = n_cur
        n_parts = []
        for pair in range(ML_H // 2):
            j0 = 4 * d + 2 * pair
            d0, d1 = dec[i][:, j0:j0 + 1], dec[i][:, j0 + 1:j0 + 2]
            c_before[i, pair] = c_cur[pair]
            c_cur[pair] = (jnp.where(_iota((128, 1), 0) < 64, d0, d1) * c_cur[pair]
                           + (contrib[i, 2 * pair] + contrib[i, 2 * pair + 1]))
            n_parts.append(jnp.where(_iota((1, 128), 1) < 64, d0, d1) * n_cur[:, 128 * pair:128 * (pair + 1)]
                           + jnp.sum(kw[i, 2 * pair] + kw[i, 2 * pair + 1], axis=0, keepdims=True))
        n_cur = jnp.concatenate(n_parts, axis=1)

    keys = [(i, hh, pair) for i in range(nchunk) for hh, pair, _ in heads]
    g_t = [(g_i[rows[i]] - bcum[rows[i]]).T for i in range(nchunk)]
    log_d, m_inter, m_t = {}, {}, {}
    for i, hh, pair in keys:
        j = 4 * d + hh
        bcol = bcum[rows[i], j:j + 1]
        log_d[i, hh] = jnp.where(mask, bcol + g_t[i][j:j + 1, :], NEG_BIG)
        m_inter[i, hh] = bcol + m_before[i][:, j:j + 1]
    for i, hh, pair in keys:
        m_t[i, hh] = jnp.maximum(m_inter[i, hh], jnp.max(log_d[i, hh], axis=1, keepdims=True))
    pw, s_int, qc, pv = {}, {}, {}, {}
    for i, hh, pair in keys:
        pw[i, hh] = jnp.where(mask, jnp.exp(log_d[i, hh] - m_t[i, hh]), 0.0) * qk[i, hh]
        s_int[i, hh] = jnp.exp(m_inter[i, hh] - m_t[i, hh])
    for i, hh, pair in keys:
        qc[i, hh] = _mm(qh[i, hh], c_before[i, pair])
        pv[i, hh] = _mm(pw[i, hh], v[rows[i], 128 * hh:128 * (hh + 1)])
    den = {}
    for i, hh, pair in keys:
        psl = slice(128 * pair, 128 * (pair + 1))
        den[i, hh] = (s_int[i, hh] * jnp.sum(qh[i, hh] * n_before[i][:, psl], axis=1, keepdims=True)
                      + jnp.sum(pw[i, hh], axis=1, keepdims=True))
    outs = [[None] * ML_H for _ in range(nchunk)]
    for i, hh, pair in keys:
        outs[i][hh] = ((s_int[i, hh] * qc[i, hh] + pv[i, hh])
                       / jnp.maximum(jnp.abs(den[i, hh]), jnp.exp(-m_t[i, hh])))
    return outs, c_cur, n_cur, m


def _mlstm_scan_kernel(fq, fg, bq, bg, bias_ref, tri_ref, hf_ref, hb_ref, c_ref, n_ref, m_ref):
    start = pl.program_id(1) == 0
    npair = ML_H // 2
    prev = [([jnp.where(start, 0.0, c_ref[d, p]) for p in range(npair)],
             jnp.where(start, 0.0, n_ref[d]), jnp.where(start, 0.0, m_ref[d])) for d in range(2)]
    res = [_mlstm_dir(fq, fg, False, 0, bias_ref, tri_ref, *prev[0]),
           _mlstm_dir(bq, bg, True, 1, bias_ref, tri_ref, *prev[1])]
    for d, h_ref in enumerate((hf_ref, hb_ref)):
        outs, c_new, n_new, m_new = res[d]
        for i, heads in enumerate(outs):
            for hh, h in enumerate(heads):
                h_ref[0, CHUNK * i:CHUNK * (i + 1), 128 * hh:128 * (hh + 1)] = h
        for p in range(npair):
            c_ref[d, p] = c_new[p]
        n_ref[d] = n_new
        m_ref[d] = m_new


def _mlstm_scan(p_ml, prm):
    bsz, s, _ = p_ml.shape
    nc = s // SCAN_ROWS
    consts = (prm["ml_bias"], prm["trib"])
    fw = lambda w, cb: pl.BlockSpec((1, SCAN_ROWS, w), lambda b, c: (b, c, cb))
    bw = lambda w, cb: pl.BlockSpec((1, SCAN_ROWS, w), lambda b, c: (b, nc - 1 - c, cb))
    return pl.pallas_call(
        _mlstm_scan_kernel,
        grid=(bsz, nc),
        in_specs=[fw(1024, 0), fw(256, 6), bw(1024, 0), bw(256, 6)] + [_full(x.shape) for x in consts],
        out_specs=[fw(MIX_W, 0), bw(MIX_W, 0)],
        out_shape=[jax.ShapeDtypeStruct((bsz, s, MIX_W), F32)] * 2,
        scratch_shapes=[pltpu.VMEM((2, ML_H // 2, 128, 128), F32), pltpu.VMEM((2, 1, 256), F32),
                        pltpu.VMEM((2, 1, 128), F32)],
        compiler_params=_params(("arbitrary", "arbitrary")),
        name="mlstm_scan",
    )(p_ml, p_ml, p_ml, p_ml, *consts)


def _lru_dir(views, first, last, rev, d, cw_ref, cb_ref, wax_ref, bax_ref, lam_ref, carry):
    m_ref, p_ref, n_ref = views
    x = m_ref[0]
    pv = jnp.where(first, 0.0, p_ref[0, HALO - 1:HALO, :])
    nv = jnp.where(last, 0.0, n_ref[0, 0:2, :])
    cw = cw_ref[...]
    xc = (_shift_rows(x, pv, None, -1) * cw[0:1] + x * cw[1:2] + _shift_rows(x, None, nv, 1) * cw[2:3]
          + _shift_rows(x, None, nv, 2) * cw[3:4] + cb_ref[...])
    rr = _mm(xc, wax_ref[d]) + bax_ref[d]
    log_a = -RG_C * _softplus(-lam_ref[d]) * _sigmoid(rr[:, :MIX_W])
    a = jnp.exp(log_a)
    mult = jnp.sqrt(jnp.maximum(1.0 - jnp.exp(2.0 * log_a), 0.0))
    bv = mult * (_sigmoid(rr[:, MIX_W:]) * xc)
    av = a

    n = x.shape[0]
    ngroups = n // 8
    av, bv = av.reshape(ngroups, 8, MIX_W), bv.reshape(ngroups, 8, MIX_W)
    row8 = _iota((1, 8, 1), 1)
    for sh in (1, 2, 4):
        shift = (8 - sh) if rev else sh
        a_sh, b_sh = pltpu.roll(av, shift, 1), pltpu.roll(bv, shift, 1)
        valid = (row8 < 8 - sh) if rev else (row8 >= sh)
        bv = jnp.where(valid, av * b_sh + bv, bv)
        av = jnp.where(valid, av * a_sh, av)
    av, bv = av.reshape(n, MIX_W), bv.reshape(n, MIX_W)
    entering = [None] * ngroups
    h = carry
    for g in (range(ngroups - 1, -1, -1) if rev else range(ngroups)):
        r = 8 * g if rev else 8 * g + 7
        entering[g] = h
        h = bv[r:r + 1] + av[r:r + 1] * h
    h_in = jnp.concatenate([jnp.broadcast_to(e, (8, MIX_W)) for e in entering], axis=0)
    return bv + av * h_in, h


def _lru_scan_kernel(nc, fm, fp, fn, bm, bp, bn, cw_ref, cb_ref, wax_ref, bax_ref, lam_ref, hf_ref, hb_ref,
                     carry_ref):
    c = pl.program_id(1)
    carry = [jnp.where(c == 0, 0.0, carry_ref[d]) for d in range(2)]
    common = (cw_ref, cb_ref, wax_ref, bax_ref, lam_ref)
    hf, cf = _lru_dir((fm, fp, fn), c == 0, c == nc - 1, False, 0, *common, carry[0])
    hb, cb = _lru_dir((bm, bp, bn), c == nc - 1, c == 0, True, 1, *common, carry[1])
    hf_ref[0] = hf
    hb_ref[0] = hb
    carry_ref[0] = cf
    carry_ref[1] = cb


def _lru_scan(p_lru, prm):
    bsz, s, _ = p_lru.shape
    nc = s // LRU_CHUNK
    consts = (prm["lru_cw"], prm["lru_cb"], prm["lru_wax"], prm["lru_bax"], prm["lru_lam"])
    out = lambda rev: pl.BlockSpec((1, LRU_CHUNK, MIX_W), (lambda b, c: (b, nc - 1 - c, 0)) if rev
                                   else (lambda b, c: (b, c, 0)))
    return pl.pallas_call(
        functools.partial(_lru_scan_kernel, nc),
        grid=(bsz, nc),
        in_specs=_seq_views(MIX_W, LRU_CHUNK, nc, 0, True) + [_full(x.shape) for x in consts],
        out_specs=[out(False), out(True)],
        out_shape=[jax.ShapeDtypeStruct((bsz, s, MIX_W), F32)] * 2,
        scratch_shapes=[pltpu.VMEM((2, 1, MIX_W), F32)],
        compiler_params=_params(("arbitrary", "arbitrary")),
        name="lru_scan",
    )(*([p_lru] * 6), *consts)


HG_LEVELS = (32, 16, 8, 4, 2, 1)


def _hgrn_lower_bound(lb_ref, layer, d):
    x = lb_ref[d]
    e = jnp.exp(x - jnp.max(x, axis=0, keepdims=True))
    sm = e / jnp.sum(e, axis=0, keepdims=True)
    acc = sm[0:1]
    for j in range(1, layer + 1):
        acc = acc + sm[j:j + 1]
    return jnp.clip(acc - sm[0:1], 0.0, 1.0)


def _hgrn_dir(q_ref, f_ref, i_ref, rev, d, layer, lb_ref, tri_ref, sel_ref, states):
    lb = _hgrn_lower_bound(lb_ref, layer, d)
    qraw, fp, v = q_ref[0], f_ref[0], i_ref[0]
    nchunk = qraw.shape[0] // CHUNK
    q = qraw * _sigmoid(qraw)
    logf = jnp.log(lb + (1.0 - lb) * _sigmoid(fp))
    kd = (1.0 - lb) * _sigmoid(-fp)
    gc = _mm_exact_l(tri_ref[d], logf)
    rows = [slice(CHUNK * i, CHUNK * (i + 1)) for i in range(nchunk)]
    tot_rows = [gc[CHUNK * i:CHUNK * i + 1] if rev else gc[CHUNK * (i + 1) - 1:CHUNK * (i + 1)]
                for i in range(nchunk)]
    gtot = jnp.concatenate([jnp.broadcast_to(t, (CHUNK, MIX_W)) for t in tot_rows], axis=0)
    q_in = q * jnp.exp(gc)
    k_out = kd * jnp.exp(gtot - gc)
    gcb = gc.astype(BF16)
    ref_chunks = [jnp.dot(sel_ref[d], gcb[rows[i]], preferred_element_type=F32) for i in range(nchunk)]
    refs = [jnp.concatenate([rc[CHUNK * lv:CHUNK * (lv + 1)] for rc in ref_chunks], axis=0)
            for lv in range(len(HG_LEVELS))]

    row = _iota((CHUNK, CHUNK), 0)
    col = _iota((CHUNK, CHUNK), 1)
    rowc = jnp.bitwise_and(_iota((q.shape[0], 1), 0), CHUNK - 1)
    qs, ks, same = [], [], []
    for lv, half in enumerate(HG_LEVELS):
        upper = lambda idx: jnp.bitwise_and(idx, 2 * half - 1) >= half
        lower = lambda idx: jnp.bitwise_and(idx, 2 * half - 1) < half
        is_q = lower(rowc) if rev else upper(rowc)
        dlt = gc - refs[lv]
        x_l = jnp.where(is_q, q, kd) * jnp.exp(jnp.where(is_q, dlt, -dlt))
        qs.append(x_l)
        ks.append(x_l)
        sh = int(math.log2(2 * half))
        q_row = lower(row) if rev else upper(row)
        k_col = upper(col) if rev else lower(col)
        same.append(jnp.logical_and(jnp.right_shift(row, sh) == jnp.right_shift(col, sh),
                                    jnp.logical_and(q_row, k_col)))

    parts, att, upd = {}, {}, {}
    for i in range(nchunk):
        for hh in range(HG_H):
            sl = slice(128 * hh, 128 * (hh + 1))
            parts[i, hh] = [_mm_nt(ql[rows[i], sl], kl[rows[i], sl]) for ql, kl in zip(qs, ks)]
            upd[i, hh] = _mm_tn(v[rows[i], sl], k_out[rows[i], sl])
    qk = q * kd
    for (i, hh), plist in parts.items():
        diag = jnp.sum(qk[rows[i], 128 * hh:128 * (hh + 1)], axis=1, keepdims=True)
        acc = jnp.where(row == col, diag, 0.0)
        for part, msk in zip(plist, same):
            acc = acc + jnp.where(msk, part, 0.0)
        att[i, hh] = acc
    cur = list(states)
    before = {}
    for i in (range(nchunk - 1, -1, -1) if rev else range(nchunk)):
        g_dec = jnp.exp(tot_rows[i])
        for hh in range(HG_H):
            before[i, hh] = cur[hh]
            cur[hh] = cur[hh] * g_dec[:, 128 * hh:128 * (hh + 1)] + upd[i, hh]
    outs = [[None] * HG_H for _ in range(nchunk)]
    for i in range(nchunk):
        for hh in range(HG_H):
            sl = slice(128 * hh, 128 * (hh + 1))
            outs[i][hh] = _mm(att[i, hh], v[rows[i], sl]) + _mm_nt(q_in[rows[i], sl], before[i, hh])
    return outs, cur


def _hgrn_scan_kernel(layer, fq, ff, fi, bq, bf, bi, lb_ref, tri_ref, sel_ref, of_ref, ob_ref, s_ref):
    start = pl.program_id(1) == 0
    prev = [[jnp.where(start, 0.0, s_ref[d, hh]) for hh in range(HG_H)] for d in range(2)]
    res = [_hgrn_dir(fq, ff, fi, False, 0, layer, lb_ref, tri_ref, sel_ref, prev[0]),
           _hgrn_dir(bq, bf, bi, True, 1, layer, lb_ref, tri_ref, sel_ref, prev[1])]
    for d, o_ref in enumerate((of_ref, ob_ref)):
        outs, new_states = res[d]
        for i, heads in enumerate(outs):
            for hh, o in enumerate(heads):
                o_ref[0, CHUNK * i:CHUNK * (i + 1), 128 * hh:128 * (hh + 1)] = o
        for hh in range(HG_H):
            s_ref[d, hh] = new_states[hh]


def _hgrn_scan(p_hg, hg_lb, layer, prm):
    bsz, s, _ = p_hg.shape
    nc = s // SCAN_ROWS
    consts = (hg_lb, prm["trib"], prm["hg_sel"])
    fw = lambda cb: pl.BlockSpec((1, SCAN_ROWS, MIX_W), lambda b, c: (b, c, cb))
    bw = lambda cb: pl.BlockSpec((1, SCAN_ROWS, MIX_W), lambda b, c: (b, nc - 1 - c, cb))
    return pl.pallas_call(
        functools.partial(_hgrn_scan_kernel, layer),
        grid=(bsz, nc),
        in_specs=[fw(0), fw(1), fw(3), bw(0), bw(2), bw(3)] + [_full(x.shape) for x in consts],
        out_specs=[fw(0), bw(0)],
        out_shape=[jax.ShapeDtypeStruct((bsz, s, MIX_W), F32)] * 2,
        scratch_shapes=[pltpu.VMEM((2, HG_H, HG_D, HG_D), F32)],
        compiler_params=_params(("arbitrary", "arbitrary")),
        name="hgrn_scan",
    )(*([p_hg] * 6), *consts)


def _head_rms128(x, g):
    parts = []
    for hh in range(MIX_W // 128):
        xh = x[:, 128 * hh:128 * (hh + 1)]
        parts.append(xh * lax.rsqrt(jnp.mean(xh * xh, axis=-1, keepdims=True) + 1e-6))
    return jnp.concatenate(parts, axis=1) * g


def _gelu_tanh(x):
    cdf = 0.5 * (1.0 + jnp.tanh(math.sqrt(2.0 / math.pi) * (x + 0.044715 * (x * x * x))))
    return x * cdf


def _merge_kernel(u_ref, h_ref, ya_ref, mlf, mlb, mlo, lrf, lrb, lrg, hgf, hgb, hgg, gt_ref, mln_ref,
                  hgn_ref, wg_ref, wb_ref, wo_ref, o_ref):
    gates = _sigmoid(jnp.dot(u_ref[0], wg_ref[...], preferred_element_type=F32))
    y_b = _sigmoid(mlo[0]) * _head_rms128(mlf[0] + mlb[0], mln_ref[...])
    y_c = (lrf[0] + lrb[0]) * _gelu_tanh(lrg[0])
    gg = hgg[0]
    y_d = (gg * _sigmoid(gg)) * _head_rms128(hgf[0] + hgb[0], hgn_ref[...])
    merged = None
    for n, y in enumerate((ya_ref[0], y_b, y_c, y_d)):
        term = gates[:, D_MODEL * n:D_MODEL * (n + 1)] * _mm(y, wb_ref[n])
        merged = term if merged is None else merged + term
    o_ref[0] = h_ref[0] + gt_ref[0, 0] * _mm(merged, wo_ref[...])


def _merge(u, h, y_a, ml, p_ml, lru, p_lru, hg, p_hg, mod_l, prm):
    bsz, s, _ = h.shape
    row = lambda w, cb=0: pl.BlockSpec((1, ROW_TILE, w), lambda b, i: (b, i, cb))
    consts = (prm["ml_norm"], prm["hg_norm"], prm["w_gate"], prm["w_branch"], prm["w_out"])
    return pl.pallas_call(
        _merge_kernel,
        grid=(bsz, s // ROW_TILE),
        in_specs=[row(D_MODEL), row(D_MODEL), row(MIX_W),
                  row(MIX_W), row(MIX_W), row(MIX_W, 2),
                  row(MIX_W), row(MIX_W), row(MIX_W, 1),
                  row(MIX_W), row(MIX_W), row(MIX_W, 4),
                  _mod_spec(2)] + [_full(x.shape) for x in consts],
        out_specs=row(D_MODEL),
        out_shape=jax.ShapeDtypeStruct((bsz, s, D_MODEL), F32),
        compiler_params=_params(("arbitrary", "arbitrary")),
        name="merge",
    )(u, h, y_a, ml[0], ml[1], p_ml, lru[0], lru[1], p_lru, hg[0], hg[1], p_hg, mod_l, *consts)


def _ffn_kernel(nt, final, hm, hp, hn, g_ref, sc_ref, sh_ref, gt_ref, wup_ref, cw_ref, cb_ref, wdn_ref,
                fg_ref, o_ref):
    i = pl.program_id(1)
    hx = jnp.concatenate([hp[0], hm[0], hn[0]], axis=0)
    n = hx.shape[0]
    row = _iota((n, 1), 0)
    valid = jnp.logical_and(jnp.logical_or(row >= HALO, i > 0),
                            jnp.logical_or(row < n - HALO, i < nt - 1))
    u2 = jnp.where(valid, _rmsnorm_rows(hx, g_ref[...]) * (1.0 + sc_ref[0, 0]) + sh_ref[0, 0], 0.0).astype(BF16)
    cw = cw_ref[...]
    cb = cb_ref[...]
    fc = D_FF // FF_SPLIT
    acc = None
    for j in range(FF_SPLIT):
        zs = []
        for base in (0, D_FF):
            sl = slice(base + fc * j, base + fc * (j + 1))
            y = jnp.dot(u2, wup_ref[:, sl], preferred_element_type=F32)
            z = (pltpu.roll(y, 1, 0) * cw[0:1, sl] + y * cw[1:2, sl] + pltpu.roll(y, n - 1, 0) * cw[2:3, sl]
                 + cb[:, sl])
            zs.append(z[HALO:n - HALO])
        val, gate = zs
        act = val * (gate * _sigmoid(gate))
        part = _mm(act, wdn_ref[fc * j:fc * (j + 1), :])
        acc = part if acc is None else acc + part
    out = hm[0] + gt_ref[0, 0] * acc
    if final:
        out = _rmsnorm_rows(out, fg_ref[...])
    o_ref[0] = out


def _conv_ffn(h, mod_l, prm, final_g, final):
    bsz, s, _ = h.shape
    nt = s // FFN_TILE
    consts = (prm["ffn_up"], prm["ffn_cw"], prm["ffn_cb"], prm["ffn_down"], final_g)
    row = pl.BlockSpec((1, FFN_TILE, D_MODEL), lambda b, i: (b, i, 0))
    return pl.pallas_call(
        functools.partial(_ffn_kernel, nt, final),
        grid=(bsz, nt),
        in_specs=_seq_views(D_MODEL, FFN_TILE, nt, 0, True)[:3]
        + [_full((1, D_MODEL)), _mod_spec(4), _mod_spec(3), _mod_spec(5)] + [_full(x.shape) for x in consts],
        out_specs=row,
        out_shape=jax.ShapeDtypeStruct((bsz, s, D_MODEL), F32),
        compiler_params=_params(("arbitrary", "arbitrary")),
        name="conv_ffn",
    )(h, h, h, prm["norm2_g"], mod_l, mod_l, mod_l, *consts)


def _block_diag(blocks):
    g, n, m = blocks.shape
    eye = jnp.eye(g, dtype=blocks.dtype)
    return (eye[:, None, :, None] * blocks[:, :, None, :]).reshape(g * n, g * m)


def _constants():
    idx = np.arange(CHUNK)
    tri = np.stack([idx[:, None] >= idx[None, :], idx[:, None] <= idx[None, :]]).astype(np.float32)
    sel = np.zeros((2, len(HG_LEVELS) * CHUNK, CHUNK), np.float32)
    for lv, half in enumerate(HG_LEVELS):
        start = (idx // (2 * half)) * (2 * half)
        sel[0, CHUNK * lv + idx, start + half - 1] = 1.0
        sel[1, CHUNK * lv + idx, start + half] = 1.0
    lane = np.arange(MIX_W)
    seg = (lane[:, None] // RW_N == lane[None, :] // RW_N).astype(np.float32)
    ridx = np.arange(SCAN_ROWS)
    same_chunk = ridx[:, None] // CHUNK == ridx[None, :] // CHUNK
    trib = np.stack([same_chunk & (ridx[:, None] >= ridx[None, :]),
                     same_chunk & (ridx[:, None] <= ridx[None, :])]).astype(np.float32)
    lanes = np.stack([np.broadcast_to(np.arange(128) < 64, (CHUNK, 128)),
                      np.broadcast_to(np.arange(128) >= 64, (CHUNK, 128))]).astype(np.float32)
    return tuple(jnp.asarray(x, BF16) for x in (tri, sel, seg, trib, lanes))


def _layer_params(l, consts, w_in, rw_mu, rw_w0, rw_w2, rw_a0, rw_a2, rw_g2, rw_kk, rw_ka, rw_rk,
                  rw_lnw, rw_lnb, ml_ibias, ml_fbias, ml_norm, lru_conv_w, lru_conv_b, lru_wa, lru_ba,
                  lru_wx, lru_bx, lru_lam, hg_norm, w_branch, w_out, norm2_g, ffn_up, ffn_conv_w,
                  ffn_conv_b, ffn_down):
    row = lambda x: x.reshape(1, -1)
    w = w_in[l]
    o_ml = RW_COLS
    o_lru = o_ml + 1552
    o_hg = o_lru + 2 * MIX_W
    o_gate = o_hg + 5 * MIX_W
    zeros = jnp.zeros((D_MODEL, 120), F32)
    w_ml = jnp.concatenate([w[:, o_ml:o_ml + 1536], w[:, o_ml + 1536:o_ml + 1544], zeros,
                            w[:, o_ml + 1544:o_ml + 1552], zeros], axis=1)
    z64 = jnp.zeros((64, MIX_W), F32)
    wlr = jnp.stack([jnp.concatenate([jnp.concatenate([rw_w2[l, d], z64], axis=1),
                                      jnp.concatenate([z64, rw_a2[l, d]], axis=1)], axis=0) for d in range(2)])
    pad = jnp.zeros((120,), F32)
    ml_bias = jnp.concatenate([ml_ibias[l].reshape(-1), pad, ml_fbias[l].reshape(-1), pad]).reshape(1, 256)
    wax = jnp.stack([jnp.concatenate([_block_diag(lru_wa[l, d]), _block_diag(lru_wx[l, d])], axis=1)
                     for d in range(2)])
    return {
        "tri": consts[0], "hg_sel": consts[1], "seg64": consts[2], "trib": consts[3], "lanes": consts[4],
        "w_rw": w[:, :RW_COLS].astype(BF16), "w_ml": w_ml.astype(BF16),
        "w_lru": w[:, o_lru:o_hg].astype(BF16), "w_hg": w[:, o_hg:o_gate].astype(BF16),
        "w_gate": w[:, o_gate:].astype(BF16),
        "mu": row(rw_mu[l]),
        "w0a0": jnp.concatenate([rw_w0[l], rw_a0[l]], axis=1).reshape(2, 1, 2 * MIX_W),
        "wlr": wlr.astype(BF16), "kk": row(rw_kk[l]), "ka": row(rw_ka[l]), "rk": row(rw_rk[l]),
        "lnw": row(rw_lnw[l]), "lnb": row(rw_lnb[l]), "g2": rw_g2[l].astype(BF16),
        "ml_bias": ml_bias, "ml_norm": row(ml_norm[l]),
        "lru_cw": lru_conv_w[l], "lru_cb": row(lru_conv_b[l]), "lru_wax": wax.astype(BF16),
        "lru_bax": jnp.concatenate([lru_ba[l], lru_bx[l]], axis=1).reshape(2, 1, 2 * MIX_W),
        "lru_lam": lru_lam[l].reshape(2, 1, MIX_W),
        "hg_norm": row(hg_norm[l]),
        "w_branch": w_branch[l].astype(BF16), "w_out": w_out[l].astype(BF16),
        "norm2_g": row(norm2_g[l]), "ffn_up": ffn_up[l].astype(BF16), "ffn_cw": ffn_conv_w[l],
        "ffn_cb": row(ffn_conv_b[l]), "ffn_down": ffn_down[l].astype(BF16),
    }


def kernel(x, c, ada_w, ada_b, norm1_g, w_in, rw_mu, rw_w0, rw_w2, rw_a0, rw_a2, rw_g2, rw_kk, rw_ka, rw_rk, rw_lnw, rw_lnb, ml_ibias, ml_fbias, ml_norm, lru_conv_w, lru_conv_b, lru_wa, lru_ba, lru_wx, lru_bx, lru_lam, hg_lb, hg_norm, w_branch, w_out, norm2_g, ffn_up, ffn_conv_w, ffn_conv_b, ffn_down, final_g):
    depth = w_in.shape[0]
    assert x.shape[1] % LRU_CHUNK == 0 and x.shape[2] == D_MODEL
    consts = _constants()
    mod = _ada_mod(c.astype(F32), ada_w, ada_b)
    h = x.astype(F32)
    final_row = final_g.reshape(1, D_MODEL)
    hg_lb = jnp.swapaxes(hg_lb.astype(F32), 0, 1)
    for l in range(depth):
        prm = _layer_params(l, consts, w_in, rw_mu, rw_w0, rw_w2, rw_a0, rw_a2, rw_g2, rw_kk, rw_ka,
                            rw_rk, rw_lnw, rw_lnb, ml_ibias, ml_fbias, ml_norm, lru_conv_w, lru_conv_b,
                            lru_wa, lru_ba, lru_wx, lru_bx, lru_lam, hg_norm, w_branch, w_out, norm2_g,
                            ffn_up, ffn_conv_w, ffn_conv_b, ffn_down)
        mod_l = mod[l]
        u, p_rw, p_ml, p_lru, p_hg = _project(h, mod_l, norm1_g[l].reshape(1, D_MODEL), prm["w_rw"],
                                              prm["w_ml"], prm["w_lru"], prm["w_hg"])
        y_a = _rwkv_finalize(p_rw, *_rwkv_scan(p_rw, prm), prm)
        ml = _mlstm_scan(p_ml, prm)
        lru = _lru_scan(p_lru, prm)
        hg = _hgrn_scan(p_hg, hg_lb, l, prm)
        h = _merge(u, h, y_a, ml, p_ml, lru, p_lru, hg, p_hg, mod_l, prm)
        h = _conv_ffn(h, mod_l, prm, final_row, l == depth - 1)
    return h
```

```python
import functools
import math

import numpy as np
import jax
import jax.numpy as jnp
from jax import lax
from jax.experimental import pallas as pl
from jax.experimental.pallas import tpu as pltpu

F32 = jnp.float32
BF16 = jnp.bfloat16

D_MODEL = 1024
MIX_W = 512
N_BRANCH = 4
RW_N = 64
RW_H = 8
RW_GN_EPS = 64e-5
RW_COLS = 1792
ML_H = 4
ML_DV = 128
ML_DK = 64
ML_COLS_PAD = 1792
LRU_CONV = 4
RG_C = 8.0
HG_H = 4
HG_D = 128
D_FF = 2816
NEG_BIG = -1e30

CHUNK = 64
BIG_CHUNK = 128
HG_CHUNK = 64
SCAN_ROWS = 256
LRU_CHUNK = 256
ROW_TILE = 256
FFN_TILE = 512
MERGE_TILE = 512
HALO = 8
FF_SPLIT = 1
VMEM_LIMIT = 56 * 1024 * 1024


def _mm(a, b):
    return jnp.dot(a.astype(BF16), b.astype(BF16), preferred_element_type=F32)


def _mm_nt(a, b):
    return lax.dot_general(a.astype(BF16), b.astype(BF16), (((1,), (1,)), ((), ())),
                           preferred_element_type=F32)


def _mm_tn(a, b):
    return lax.dot_general(a.astype(BF16), b.astype(BF16), (((0,), (0,)), ((), ())),
                           preferred_element_type=F32)


def _split3(x):
    hi = x.astype(BF16)
    r = x - hi.astype(F32)
    mid = r.astype(BF16)
    lo = (r - mid.astype(F32)).astype(BF16)
    return hi, mid, lo


def _mm_exact_l(sel, x):
    n = x.shape[1]
    r = jnp.dot(sel, jnp.concatenate(_split3(x), axis=1), preferred_element_type=F32)
    return r[:, :n] + r[:, n:2 * n] + r[:, 2 * n:]


def _sigmoid(x):
    return jax.nn.sigmoid(x)


def _softplus(x):
    return jnp.maximum(x, 0.0) + jnp.log(1.0 + jnp.exp(-jnp.abs(x)))


def _rmsnorm_rows(x, g, eps=1e-6):
    return x * lax.rsqrt(jnp.mean(x * x, axis=-1, keepdims=True) + eps) * g


def _iota(shape, dim):
    return lax.broadcasted_iota(jnp.int32, shape, dim)


def _shift_rows(x, prev_row, next_row, k):
    n = x.shape[0]
    row = _iota((n, 1), 0)
    rolled = pltpu.roll(x, (-k) % n, 0)
    if k == -1:
        return jnp.where(row == 0, prev_row, rolled)
    out = rolled
    for j in range(k):
        out = jnp.where(row == n - k + j, next_row[j:j + 1], out)
    return out


def _params(sem):
    return pltpu.CompilerParams(dimension_semantics=sem, vmem_limit_bytes=VMEM_LIMIT)


def _full(shape):
    nd = len(shape)
    return pl.BlockSpec(shape, lambda *_: (0,) * nd)


def _resident(shape):
    nd = len(shape)
    return pl.BlockSpec(shape, lambda *_: (0,) * nd, pipeline_mode=pl.Buffered(1))


def _seq_views(width, rows, nsteps, col_block, halo):
    per = rows // HALO
    last = nsteps * per - 1
    specs = []
    for rev in (False, True):
        idx = (lambda c: nsteps - 1 - c) if rev else (lambda c: c)
        specs.append(pl.BlockSpec((1, rows, width), lambda b, c, idx=idx: (b, idx(c), col_block)))
        if halo:
            specs.append(pl.BlockSpec(
                (1, HALO, width), lambda b, c, idx=idx: (b, jnp.maximum(idx(c) * per - 1, 0), col_block)))
            specs.append(pl.BlockSpec(
                (1, HALO, width), lambda b, c, idx=idx: (b, jnp.minimum((idx(c) + 1) * per, last), col_block)))
    return specs


def _tri_masks(rev):
    row = _iota((CHUNK, 128), 0)
    col = jnp.bitwise_and(_iota((CHUNK, 128), 1), CHUNK - 1)
    if rev:
        strict, incl = col > row, col >= row
    else:
        strict, incl = col < row, col <= row
    blk = jnp.right_shift(col, 4) == jnp.right_shift(row, 4)
    return strict, incl, blk, col == row


def _ada_kernel(c_ref, w_ref, b_ref, o_ref):
    c = c_ref[...]
    cond = c * _sigmoid(c)
    o_ref[0, 0] = jnp.dot(cond, w_ref[0], preferred_element_type=F32,
                          precision=lax.Precision.HIGHEST) + b_ref[0, 0]


def _ada_mod(c, ada_w, ada_b):
    depth = ada_w.shape[0]
    bsz = c.shape[0]
    out = pl.pallas_call(
        _ada_kernel,
        grid=(depth, 6),
        in_specs=[_full((bsz, D_MODEL)),
                  pl.BlockSpec((1, D_MODEL, D_MODEL), lambda l, j: (l, 0, j)),
                  pl.BlockSpec((1, 1, 1, D_MODEL), lambda l, j: (l, j, 0, 0))],
        out_specs=pl.BlockSpec((1, 1, bsz, D_MODEL), lambda l, j: (l, j, 0, 0)),
        out_shape=jax.ShapeDtypeStruct((depth, 6, bsz, D_MODEL), F32),
        compiler_params=_params(("arbitrary", "arbitrary")),
        name="ada_mod",
    )(c, ada_w, ada_b.reshape(depth, 6, 1, D_MODEL))
    return out.reshape(depth, 6, bsz, 1, D_MODEL)


def _mod_spec(idx):
    return pl.BlockSpec((1, 1, 1, D_MODEL), lambda b, i: (idx, b, 0, 0))


def _proj_kernel(h_ref, g_ref, sc_ref, sh_ref, wrw, wml, wlru, whg, u_ref, prw, pml, plru, phg):
    u = _rmsnorm_rows(h_ref[0], g_ref[...]) * (1.0 + sc_ref[0, 0]) + sh_ref[0, 0]
    ub = u.astype(BF16)
    u_ref[0] = ub
    for w_ref, o_ref in ((wrw, prw), (wml, pml), (wlru, plru), (whg, phg)):
        o_ref[0] = jnp.dot(ub, w_ref[...], preferred_element_type=F32)


def _project(h, mod_l, g1, wrw, wml, wlru, whg):
    bsz, s, _ = h.shape
    widths = [w.shape[1] for w in (wrw, wml, wlru, whg)]
    row = lambda w: pl.BlockSpec((1, ROW_TILE, w), lambda b, i: (b, i, 0))
    return pl.pallas_call(
        _proj_kernel,
        grid=(bsz, s // ROW_TILE),
        in_specs=[row(D_MODEL), _full((1, D_MODEL)), _mod_spec(1), _mod_spec(0)]
        + [_full(w.shape) for w in (wrw, wml, wlru, whg)],
        out_specs=[row(D_MODEL)] + [row(w) for w in widths],
        out_shape=[jax.ShapeDtypeStruct((bsz, s, D_MODEL), BF16)]
        + [jax.ShapeDtypeStruct((bsz, s, w), F32) for w in widths],
        compiler_params=_params(("arbitrary", "arbitrary")),
        name="in_proj",
    )(h, g1, mod_l, mod_l, wrw, wml, wlru, whg)


def _rwkv_shifted(m_ref, p_ref, n_ref, first, last, mu):
    x = m_ref[0]
    pv = jnp.where(first, 0.0, p_ref[0, HALO - 1:HALO, :])
    nv = jnp.where(last, 0.0, n_ref[0, 0:1, :])
    prev = _shift_rows(x, pv, None, -1)
    nxt = _shift_rows(x, None, nv, 1)
    return x + mu * (0.5 * (prev + nxt) - x)


def _rwkv_lowrank(xs, d, w0a0_ref, wlr_ref):
    blk = xs[:, 1536:1664]
    lane = _iota(blk.shape, 1)
    lrin = jnp.where(lane < 64, jnp.tanh(blk), blk)
    return _mm(lrin, wlr_ref[d]) + w0a0_ref[d]


def _rwkv_dir(views, first, last, rev, d, mu_ref, w0a0_ref, wlr_ref, kk_ref, ka_ref, seg_ref, tri_ref,
              lanes_ref):
    xs = _rwkv_shifted(*views, first, last, mu_ref[...])
    r, k, v = xs[:, 0:512], xs[:, 512:1024], xs[:, 1024:1536]
    lr = _rwkv_lowrank(xs, d, w0a0_ref, wlr_ref)
    lw = -jnp.exp(-_softplus(-lr[:, :MIX_W]) - 0.5)
    a = _sigmoid(lr[:, MIX_W:])
    kkm = k * kk_ref[...]
    ss = _mm(kkm * kkm, seg_ref[...])
    kkn = kkm * lax.rsqrt(jnp.maximum(ss, 1e-24))
    a_s, b_s = -kkn, kkn * a
    k_s = k * (1.0 + (a - 1.0) * ka_ref[...])

    cum = _mm_exact_l(tri_ref[d], lw)
    nchunk = xs.shape[0] // CHUNK
    tot_rows = [cum[CHUNK * i:CHUNK * i + 1] if rev else cum[CHUNK * (i + 1) - 1:CHUNK * (i + 1)]
                for i in range(nchunk)]
    ctot = jnp.concatenate([jnp.broadcast_to(t, (CHUNK, MIX_W)) for t in tot_rows], axis=0)
    e_neg = jnp.exp(-cum)
    e_hat = jnp.exp(ctot - cum)
    at, rt = a_s * jnp.exp(cum - lw), r * jnp.exp(cum)
    bt, kt = b_s * e_neg, k_s * e_neg
    bh, kh = b_s * e_hat, k_s * e_hat

    strict, incl, blk16, eye = _tri_masks(rev)
    brow = _iota((128, 128), 0) < 64
    bcol = _iota((128, 128), 1) < 64
    bdmask = brow == bcol
    masks = (strict, incl, blk16, eye, bdmask, lanes_ref[0], lanes_ref[1])
    chains = []
    for i in range(nchunk):
        rs = slice(CHUNK * i, CHUNK * (i + 1))
        gam = jnp.exp(tot_rows[i])
        chains.append([_rwkv_pair_free(at[rs, sl], rt[rs, sl], bt[rs, sl], kt[rs, sl], bh[rs, sl], kh[rs, sl],
                                       v[rs, sl], gam[:, sl], masks)
                       for sl in (slice(128 * j, 128 * (j + 1)) for j in range(RW_H // 2))])
    return chains


def _rwkv_pair_free(at, rt, bt, kt, bh, kh, vj, gam, masks):
    strict, incl, blk16, eye, bdmask, lo, hi = masks
    b16 = lambda x: x.astype(BF16)
    bdr = lambda xb: jnp.concatenate([xb * lo, xb * hi], axis=0)
    dot = lambda l, r: jnp.dot(l, r, preferred_element_type=F32)
    ar = b16(jnp.concatenate([at, rt], axis=0))
    bk = jnp.concatenate([bdr(b16(bt)), bdr(b16(kt))], axis=0)
    sc = _mm_nt(ar, bk)
    yield
    a_ab = jnp.where(strict, sc[:CHUNK, :128], 0.0)
    a_ak = jnp.where(strict, sc[:CHUNK, 128:], 0.0)
    r_b = jnp.where(incl, sc[CHUNK:, :128], 0.0)
    r_k = jnp.where(incl, sc[CHUNK:, 128:], 0.0)
    rows2 = lambda x, y: b16(jnp.concatenate([x, y], axis=0))
    a_d = jnp.where(blk16, a_ab, 0.0)
    a_o = a_ab - a_d
    a_d16, vbd = b16(a_d), bdr(b16(vj))
    a2 = dot(a_d16, bdr(a_d16))
    akrk = dot(rows2(a_ak, r_k), vbd)
    yield
    t = eye.astype(F32) + a_d
    prod = dot(rows2(t, a2), bdr(b16(a2)))
    t, a4 = t + prod[:CHUNK], prod[CHUNK:]
    yield
    prod = dot(rows2(t, a4), bdr(b16(a4)))
    t, a8 = t + prod[:CHUNK], prod[CHUNK:]
    yield
    t = t + dot(b16(t), bdr(b16(a8)))
    yield
    m = dot(b16(a_o), bdr(b16(t)))
    yield
    prod = dot(rows2(t, m), bdr(b16(m)))
    t, m2 = t + prod[:CHUNK], prod[CHUNK:]
    yield
    t = t + dot(b16(t), bdr(b16(m2)))
    yield
    return dict(ar=ar, t=b16(t), avk=akrk[:CHUNK], rkv=akrk[CHUNK:], rb=b16(r_b), vj=b16(vj),
                bhk=b16(jnp.concatenate([bh, kh], axis=0)), gam=gam, bdmask=bdmask, lo=lo, hi=hi)


def _rwkv_pair_carry(free, sbd):
    lo, hi = free["lo"], free["hi"]
    bdr = lambda xb: jnp.concatenate([xb * lo, xb * hi], axis=0)
    ars = _mm_nt(free["ar"], sbd)
    yield
    u = jnp.dot(free["t"], bdr((ars[:CHUNK] + free["avk"]).astype(BF16)), preferred_element_type=F32)
    yield
    u16 = u.astype(BF16)
    y = ars[CHUNK:] + free["rkv"] + jnp.dot(free["rb"], bdr(u16), preferred_element_type=F32)
    upd = _mm_tn(jnp.concatenate([u16, free["vj"]], axis=0), free["bhk"])
    return y, free["gam"] * sbd + jnp.where(free["bdmask"], upd, 0.0)


def _run_round_robin(chains):
    results = [None] * len(chains)
    active = list(range(len(chains)))
    while active:
        for i in list(active):
            try:
                next(chains[i])
            except StopIteration as stop:
                results[i] = stop.value
                active.remove(i)
    return results


def _rwkv_scan_kernel(nc, fm, fp, fn, bm, bp, bn, mu_ref, w0a0_ref, wlr_ref, kk_ref, ka_ref, seg_ref,
                      tri_ref, lanes_ref, yf_ref, yb_ref, sf_ref, sb_ref):
    c = pl.program_id(1)
    npair = RW_H // 2
    nchunk = SCAN_ROWS // CHUNK
    states = [[jnp.where(c == 0, 0.0, s_ref[j]) for j in range(npair)] for s_ref in (sf_ref, sb_ref)]
    common = (mu_ref, w0a0_ref, wlr_ref, kk_ref, ka_ref, seg_ref, tri_ref, lanes_ref)
    free_f = _rwkv_dir((fm, fp, fn), c == 0, c == nc - 1, False, 0, *common)
    free_b = _rwkv_dir((bm, bp, bn), c == nc - 1, c == 0, True, 1, *common)
    res = _run_round_robin([g for chunk in free_f + free_b for g in chunk])
    free = [res[:nchunk * npair], res[nchunk * npair:]]
    y_refs = (yf_ref, yb_ref)
    outs = []
    for step in range(nchunk):
        chunk_of = (step, nchunk - 1 - step)
        res = _run_round_robin([_rwkv_pair_carry(free[dd][chunk_of[dd] * npair + j], states[dd][j])
                                for dd in range(2) for j in range(npair)])
        for dd in range(2):
            for j in range(npair):
                y, states[dd][j] = res[dd * npair + j]
                outs.append((dd, chunk_of[dd], j, y))
    for dd, i, j, y in outs:
        y_refs[dd][0, CHUNK * i:CHUNK * (i + 1), 128 * j:128 * (j + 1)] = y
    for dd, s_ref in enumerate((sf_ref, sb_ref)):
        for j in range(npair):
            s_ref[j] = states[dd][j]


def _rwkv_scan(p_rw, prm):
    bsz, s, _ = p_rw.shape
    nc = s // SCAN_ROWS
    consts = (prm["mu"], prm["w0a0"], prm["wlr"], prm["kk"], prm["ka"], prm["seg64"], prm["trib"],
              prm["lanes"])
    out = lambda rev: pl.BlockSpec((1, SCAN_ROWS, MIX_W), (lambda b, c: (b, nc - 1 - c, 0)) if rev
                                   else (lambda b, c: (b, c, 0)))
    return pl.pallas_call(
        functools.partial(_rwkv_scan_kernel, nc),
        grid=(bsz, nc),
        in_specs=_seq_views(RW_COLS, SCAN_ROWS, nc, 0, True) + [_full(x.shape) for x in consts],
        out_specs=[out(False), out(True)],
        out_shape=[jax.ShapeDtypeStruct((bsz, s, MIX_W), F32)] * 2,
        scratch_shapes=[pltpu.VMEM((RW_H // 2, 128, 128), F32)] * 2,
        compiler_params=_params(("arbitrary", "arbitrary")),
        name="rwkv_scan",
    )(*([p_rw] * 6), *consts)


def _rwkv_fin_kernel(nt, pm, pp, pn, yf_ref, yb_ref, mu_ref, w0a0_ref, wlr_ref, ka_ref, rk_ref, lnw_ref,
                     lnb_ref, g2_ref, seg_ref, o_ref):
    i = pl.program_id(1)
    xs = _rwkv_shifted(pm, pp, pn, i == 0, i == nt - 1, mu_ref[...])
    r, k, v = xs[:, 0:512], xs[:, 512:1024], xs[:, 1024:1536]
    kts = []
    for d in range(2):
        a = _sigmoid(_rwkv_lowrank(xs, d, w0a0_ref, wlr_ref)[:, MIX_W:])
        kts.append(k * (1.0 + (a - 1.0) * ka_ref[...]))
    y = yf_ref[0] + yb_ref[0]
    inv_n = 1.0 / RW_N
    mean = _mm(y, seg_ref[...]) * inv_n
    yc = y - mean
    var = _mm(yc * yc, seg_ref[...]) * inv_n
    y = yc * lax.rsqrt(var + RW_GN_EPS) * lnw_ref[...] + lnb_ref[...]
    k_bonus = 0.5 * (kts[0] + kts[1])
    bonus = _mm(r * k_bonus * rk_ref[...], seg_ref[...]) * v
    g = _mm(_sigmoid(xs[:, 1664:1792]), g2_ref[...])
    o_ref[0] = ((y + bonus) * g).astype(BF16)


def _rwkv_finalize(p_rw, yf, yb, prm):
    bsz, s, _ = p_rw.shape
    nt = s // ROW_TILE
    consts = (prm["mu"], prm["w0a0"], prm["wlr"], prm["ka"], prm["rk"], prm["lnw"], prm["lnb"], prm["g2"],
              prm["seg64"])
    row = pl.BlockSpec((1, ROW_TILE, MIX_W), lambda b, i: (b, i, 0))
    return pl.pallas_call(
        functools.partial(_rwkv_fin_kernel, nt),
        grid=(bsz, nt),
        in_specs=_seq_views(RW_COLS, ROW_TILE, nt, 0, True)[:3] + [row, row] + [_full(x.shape) for x in consts],
        out_specs=row,
        out_shape=jax.ShapeDtypeStruct((bsz, s, MIX_W), BF16),
        compiler_params=_params(("arbitrary", "arbitrary")),
        name="rwkv_finalize",
    )(p_rw, p_rw, p_rw, yf, yb, *consts)


def _mlstm_dir(qkv_ref, gate_ref, rev, d, bias_ref, tri_ref, c_prev, n_prev, m_prev):
    qkv = qkv_ref[0]
    nchunk = qkv.shape[0] // BIG_CHUNK
    q = qkv[:, 0:256] * (ML_DK ** -0.5)
    k = qkv[:, 256:512]
    v = qkv[:, 512:1024]
    gates = gate_ref[0] + bias_ref[...]
    g_i = gates[:, 0:128]
    g_f = -_softplus(-gates[:, 128:256])
    bcum = _mm_exact_l(tri_ref[d], g_f)
    order = range(nchunk - 1, -1, -1) if rev else range(nchunk)
    rows = [slice(BIG_CHUNK * i, BIG_CHUNK * (i + 1)) for i in range(nchunk)]

    m = m_prev
    m_before, w_c, dec = [None] * nchunk, [None] * nchunk, [None] * nchunk
    for i in order:
        g_end = (bcum[BIG_CHUNK * i:BIG_CHUNK * i + 1] if rev
                 else bcum[BIG_CHUNK * (i + 1) - 1:BIG_CHUNK * (i + 1)])
        a_end = g_end - bcum[rows[i]] + g_i[rows[i]]
        m_new = jnp.maximum(g_end + m, jnp.max(a_end, axis=0, keepdims=True))
        w_c[i] = jnp.exp(a_end - m_new)
        dec[i] = jnp.exp(g_end + m - m_new)
        m_before[i] = m
        m = m_new

    row = _iota((BIG_CHUNK, BIG_CHUNK), 0)
    col = _iota((BIG_CHUNK, BIG_CHUNK), 1)
    mask = (col >= row) if rev else (col <= row)
    lane_lo = _iota((BIG_CHUNK, 128), 1) < 64
    heads = [(hh,) + divmod(hh, 2) for hh in range(ML_H)]

    qh, qk, contrib, kw = {}, {}, {}, {}
    for i in range(nchunk):
        for hh, pair, half in heads:
            psl = slice(128 * pair, 128 * (pair + 1))
            j = 4 * d + hh
            hm = lane_lo if half == 0 else jnp.logical_not(lane_lo)
            kp = k[rows[i], psl]
            qh[i, hh] = jnp.where(hm, q[rows[i], psl], 0.0)
            kw[i, hh] = jnp.where(hm, kp, 0.0) * w_c[i][:, j:j + 1]
    for i in range(nchunk):
        for hh, pair, half in heads:
            qk[i, hh] = _mm_nt(qh[i, hh], k[rows[i], 128 * pair:128 * (pair + 1)])
            contrib[i, hh] = _mm_tn(kw[i, hh], v[rows[i], 128 * hh:128 * (hh + 1)])

    c_cur, n_cur = list(c_prev), n_prev
    c_before, n_before = {}, {}
    for i in order:
        n_before[i] = n_cur
        n_parts = []
        for pair in range(ML_H // 2):
            j0 = 4 * d + 2 * pair
            d0, d1 = dec[i][:, j0:j0 + 1], dec[i][:, j0 + 1:j0 + 2]
            c_before[i, pair] = c_cur[pair]
            c_cur[pair] = (jnp.where(_iota((128, 1), 0) < 64, d0, d1) * c_cur[pair]
                           + (contrib[i, 2 * pair] + contrib[i, 2 * pair + 1]))
            n_parts.append(jnp.where(_iota((1, 128), 1) < 64, d0, d1) * n_cur[:, 128 * pair:128 * (pair + 1)]
                           + jnp.sum(kw[i, 2 * pair] + kw[i, 2 * pair + 1], axis=0, keepdims=True))
        n_cur = jnp.concatenate(n_parts, axis=1)

    keys = [(i, hh, pair) for i in range(nchunk) for hh, pair, _ in heads]
    g_t = [(g_i[rows[i]] - bcum[rows[i]]).T for i in range(nchunk)]
    log_d, m_inter, m_t = {}, {}, {}
    for i, hh, pair in keys:
        j = 4 * d + hh
        bcol = bcum[rows[i], j:j + 1]
        log_d[i, hh] = jnp.where(mask, bcol + g_t[i][j:j + 1, :], NEG_BIG)
        m_inter[i, hh] = bcol + m_before[i][:, j:j + 1]
    for i, hh, pair in keys:
        m_t[i, hh] = jnp.maximum(m_inter[i, hh], jnp.max(log_d[i, hh], axis=1, keepdims=True))
    pw, s_int, qc, pv = {}, {}, {}, {}
    for i, hh, pair in keys:
        pw[i, hh] = jnp.where(mask, jnp.exp(log_d[i, hh] - m_t[i, hh]), 0.0) * qk[i, hh]
        s_int[i, hh] = jnp.exp(m_inter[i, hh] - m_t[i, hh])
    for i, hh, pair in keys:
        qc[i, hh] = _mm(qh[i, hh], c_before[i, pair])
        pv[i, hh] = _mm(pw[i, hh], v[rows[i], 128 * hh:128 * (hh + 1)])
    den = {}
    for i, hh, pair in keys:
        psl = slice(128 * pair, 128 * (pair + 1))
        den[i, hh] = (s_int[i, hh] * jnp.sum(qh[i, hh] * n_before[i][:, psl], axis=1, keepdims=True)
                      + jnp.sum(pw[i, hh], axis=1, keepdims=True))
    outs = [[None] * ML_H for _ in range(nchunk)]
    for i, hh, pair in keys:
        outs[i][hh] = ((s_int[i, hh] * qc[i, hh] + pv[i, hh])
                       / jnp.maximum(jnp.abs(den[i, hh]), jnp.exp(-m_t[i, hh])))
    return outs, c_cur, n_cur, m


def _mlstm_scan_kernel(fq, fg, bq, bg, bias_ref, tri_ref, hf_ref, hb_ref, c_ref, n_ref, m_ref):
    start = pl.program_id(1) == 0
    npair = ML_H // 2
    prev = [([jnp.where(start, 0.0, c_ref[d, p]) for p in range(npair)],
             jnp.where(start, 0.0, n_ref[d]), jnp.where(start, 0.0, m_ref[d])) for d in range(2)]
    res = [_mlstm_dir(fq, fg, False, 0, bias_ref, tri_ref, *prev[0]),
           _mlstm_dir(bq, bg, True, 1, bias_ref, tri_ref, *prev[1])]
    for d, h_ref in enumerate((hf_ref, hb_ref)):
        outs, c_new, n_new, m_new = res[d]
        for i, heads in enumerate(outs):
            for hh, h in enumerate(heads):
                h_ref[0, BIG_CHUNK * i:BIG_CHUNK * (i + 1), 128 * hh:128 * (hh + 1)] = h
        for p in range(npair):
            c_ref[d, p] = c_new[p]
        n_ref[d] = n_new
        m_ref[d] = m_new


def _mlstm_scan(p_ml, prm):
    bsz, s, _ = p_ml.shape
    nc = s // SCAN_ROWS
    consts = (prm["ml_bias"], prm["trib_big"])
    fw = lambda w, cb: pl.BlockSpec((1, SCAN_ROWS, w), lambda b, c: (b, c, cb))
    bw = lambda w, cb: pl.BlockSpec((1, SCAN_ROWS, w), lambda b, c: (b, nc - 1 - c, cb))
    return pl.pallas_call(
        _mlstm_scan_kernel,
        grid=(bsz, nc),
        in_specs=[fw(1024, 0), fw(256, 6), bw(1024, 0), bw(256, 6)] + [_full(x.shape) for x in consts],
        out_specs=[fw(MIX_W, 0), bw(MIX_W, 0)],
        out_shape=[jax.ShapeDtypeStruct((bsz, s, MIX_W), F32)] * 2,
        scratch_shapes=[pltpu.VMEM((2, ML_H // 2, 128, 128), F32), pltpu.VMEM((2, 1, 256), F32),
                        pltpu.VMEM((2, 1, 128), F32)],
        compiler_params=_params(("arbitrary", "arbitrary")),
        name="mlstm_scan",
    )(p_ml, p_ml, p_ml, p_ml, *consts)


def _lru_dir(views, first, last, rev, d, cw_ref, cb_ref, wax_ref, bax_ref, lam_ref, carry):
    m_ref, p_ref, n_ref = views
    x = m_ref[0]
    pv = jnp.where(first, 0.0, p_ref[0, HALO - 1:HALO, :])
    nv = jnp.where(last, 0.0, n_ref[0, 0:2, :])
    cw = cw_ref[...]
    xc = (_shift_rows(x, pv, None, -1) * cw[0:1] + x * cw[1:2] + _shift_rows(x, None, nv, 1) * cw[2:3]
          + _shift_rows(x, None, nv, 2) * cw[3:4] + cb_ref[...])
    rr = _mm(xc, wax_ref[d]) + bax_ref[d]
    log_a = -RG_C * _softplus(-lam_ref[d]) * _sigmoid(rr[:, :MIX_W])
    a = jnp.exp(log_a)
    mult = jnp.sqrt(jnp.maximum(1.0 - jnp.exp(2.0 * log_a), 0.0))
    bv = mult * (_sigmoid(rr[:, MIX_W:]) * xc)
    av = a

    n = x.shape[0]
    ngroups = n // 8
    av, bv = av.reshape(ngroups, 8, MIX_W), bv.reshape(ngroups, 8, MIX_W)
    row8 = _iota((1, 8, 1), 1)
    for sh in (1, 2, 4):
        shift = (8 - sh) if rev else sh
        a_sh, b_sh = pltpu.roll(av, shift, 1), pltpu.roll(bv, shift, 1)
        valid = (row8 < 8 - sh) if rev else (row8 >= sh)
        bv = jnp.where(valid, av * b_sh + bv, bv)
        av = jnp.where(valid, av * a_sh, av)
    av, bv = av.reshape(n, MIX_W), bv.reshape(n, MIX_W)
    entering = [None] * ngroups
    h = carry
    for g in (range(ngroups - 1, -1, -1) if rev else range(ngroups)):
        r = 8 * g if rev else 8 * g + 7
        entering[g] = h
        h = bv[r:r + 1] + av[r:r + 1] * h
    h_in = jnp.concatenate([jnp.broadcast_to(e, (8, MIX_W)) for e in entering], axis=0)
    return bv + av * h_in, h


def _lru_scan_kernel(nc, fm, fp, fn, bm, bp, bn, cw_ref, cb_ref, wax_ref, bax_ref, lam_ref, hf_ref, hb_ref,
                     carry_ref):
    c = pl.program_id(1)
    carry = [jnp.where(c == 0, 0.0, carry_ref[d]) for d in range(2)]
    common = (cw_ref, cb_ref, wax_ref, bax_ref, lam_ref)
    hf, cf = _lru_dir((fm, fp, fn), c == 0, c == nc - 1, False, 0, *common, carry[0])
    hb, cb = _lru_dir((bm, bp, bn), c == nc - 1, c == 0, True, 1, *common, carry[1])
    hf_ref[0] = hf
    hb_ref[0] = hb
    carry_ref[0] = cf
    carry_ref[1] = cb


def _lru_scan(p_lru, prm):
    bsz, s, _ = p_lru.shape
    nc = s // LRU_CHUNK
    consts = (prm["lru_cw"], prm["lru_cb"], prm["lru_wax"], prm["lru_bax"], prm["lru_lam"])
    out = lambda rev: pl.BlockSpec((1, LRU_CHUNK, MIX_W), (lambda b, c: (b, nc - 1 - c, 0)) if rev
                                   else (lambda b, c: (b, c, 0)))
    return pl.pallas_call(
        functools.partial(_lru_scan_kernel, nc),
        grid=(bsz, nc),
        in_specs=_seq_views(MIX_W, LRU_CHUNK, nc, 0, True) + [_full(x.shape) for x in consts],
        out_specs=[out(False), out(True)],
        out_shape=[jax.ShapeDtypeStruct((bsz, s, MIX_W), F32)] * 2,
        scratch_shapes=[pltpu.VMEM((2, 1, MIX_W), F32)],
        compiler_params=_params(("arbitrary", "arbitrary")),
        name="lru_scan",
    )(*([p_lru] * 6), *consts)


HG_LEVELS = (32, 16, 8, 4, 2, 1)


def _hgrn_lower_bound(lb_ref, layer, d):
    x = lb_ref[d]
    e = jnp.exp(x - jnp.max(x, axis=0, keepdims=True))
    sm = e / jnp.sum(e, axis=0, keepdims=True)
    acc = sm[0:1]
    for j in range(1, layer + 1):
        acc = acc + sm[j:j + 1]
    return jnp.clip(acc - sm[0:1], 0.0, 1.0)


def _hgrn_dir(q_ref, f_ref, i_ref, rev, d, layer, lb_ref, tri_ref, sel_ref, states):
    lb = _hgrn_lower_bound(lb_ref, layer, d)
    qraw, fp, v = q_ref[0], f_ref[0], i_ref[0]
    nchunk = qraw.shape[0] // HG_CHUNK
    q = qraw * _sigmoid(qraw)
    logf = jnp.log(lb + (1.0 - lb) * _sigmoid(fp))
    kd = (1.0 - lb) * _sigmoid(-fp)
    gc = _mm_exact_l(tri_ref[d], logf)
    rows = [slice(HG_CHUNK * i, HG_CHUNK * (i + 1)) for i in range(nchunk)]
    tot_rows = [gc[HG_CHUNK * i:HG_CHUNK * i + 1] if rev else gc[HG_CHUNK * (i + 1) - 1:HG_CHUNK * (i + 1)]
                for i in range(nchunk)]
    gtot = jnp.concatenate([jnp.broadcast_to(t, (HG_CHUNK, MIX_W)) for t in tot_rows], axis=0)
    q_in = q * jnp.exp(gc)
    k_out = kd * jnp.exp(gtot - gc)
    gcb = gc.astype(BF16)
    ref_chunks = [jnp.dot(sel_ref[d], gcb[rows[i]], preferred_element_type=F32) for i in range(nchunk)]
    refs = [jnp.concatenate([rc[HG_CHUNK * lv:HG_CHUNK * (lv + 1)] for rc in ref_chunks], axis=0)
            for lv in range(len(HG_LEVELS))]

    row = _iota((HG_CHUNK, HG_CHUNK), 0)
    col = _iota((HG_CHUNK, HG_CHUNK), 1)
    rowc = jnp.bitwise_and(_iota((q.shape[0], 1), 0), HG_CHUNK - 1)
    qs, ks, same = [], [], []
    for lv, half in enumerate(HG_LEVELS):
        upper = lambda idx: jnp.bitwise_and(idx, 2 * half - 1) >= half
        lower = lambda idx: jnp.bitwise_and(idx, 2 * half - 1) < half
        is_q = lower(rowc) if rev else upper(rowc)
        dlt = gc - refs[lv]
        x_l = jnp.where(is_q, q, kd) * jnp.exp(jnp.where(is_q, dlt, -dlt))
        qs.append(x_l)
        ks.append(x_l)
        sh = int(math.log2(2 * half))
        q_row = lower(row) if rev else upper(row)
        k_col = upper(col) if rev else lower(col)
        same.append(jnp.logical_and(jnp.right_shift(row, sh) == jnp.right_shift(col, sh),
                                    jnp.logical_and(q_row, k_col)))

    parts, att, upd = {}, {}, {}
    for i in range(nchunk):
        for hh in range(HG_H):
            sl = slice(128 * hh, 128 * (hh + 1))
            parts[i, hh] = [_mm_nt(ql[rows[i], sl], kl[rows[i], sl]) for ql, kl in zip(qs, ks)]
            upd[i, hh] = _mm_tn(v[rows[i], sl], k_out[rows[i], sl])
    qk = q * kd
    for (i, hh), plist in parts.items():
        diag = jnp.sum(qk[rows[i], 128 * hh:128 * (hh + 1)], axis=1, keepdims=True)
        acc = jnp.where(row == col, diag, 0.0)
        for part, msk in zip(plist, same):
            acc = acc + jnp.where(msk, part, 0.0)
        att[i, hh] = acc
    cur = list(states)
    before = {}
    for i in (range(nchunk - 1, -1, -1) if rev else range(nchunk)):
        g_dec = jnp.exp(tot_rows[i])
        for hh in range(HG_H):
            before[i, hh] = cur[hh]
            cur[hh] = cur[hh] * g_dec[:, 128 * hh:128 * (hh + 1)] + upd[i, hh]
    outs = [[None] * HG_H for _ in range(nchunk)]
    for i in range(nchunk):
        for hh in range(HG_H):
            sl = slice(128 * hh, 128 * (hh + 1))
            outs[i][hh] = _mm(att[i, hh], v[rows[i], sl]) + _mm_nt(q_in[rows[i], sl], before[i, hh])
    return outs, cur


def _hgrn_scan_kernel(layer, fq, ff, fi, bq, bf, bi, lb_ref, tri_ref, sel_ref, of_ref, ob_ref, s_ref):
    start = pl.program_id(1) == 0
    prev = [[jnp.where(start, 0.0, s_ref[d, hh]) for hh in range(HG_H)] for d in range(2)]
    res = [_hgrn_dir(fq, ff, fi, False, 0, layer, lb_ref, tri_ref, sel_ref, prev[0]),
           _hgrn_dir(bq, bf, bi, True, 1, layer, lb_ref, tri_ref, sel_ref, prev[1])]
    for d, o_ref in enumerate((of_ref, ob_ref)):
        outs, new_states = res[d]
        for i, heads in enumerate(outs):
            for hh, o in enumerate(heads):
                o_ref[0, HG_CHUNK * i:HG_CHUNK * (i + 1), 128 * hh:128 * (hh + 1)] = o
        for hh in range(HG_H):
            s_ref[d, hh] = new_states[hh]


def _hgrn_scan(p_hg, hg_lb, layer, prm):
    bsz, s, _ = p_hg.shape
    nc = s // SCAN_ROWS
    consts = (hg_lb, prm["trib"], prm["hg_sel"])
    fw = lambda cb: pl.BlockSpec((1, SCAN_ROWS, MIX_W), lambda b, c: (b, c, cb))
    bw = lambda cb: pl.BlockSpec((1, SCAN_ROWS, MIX_W), lambda b, c: (b, nc - 1 - c, cb))
    return pl.pallas_call(
        functools.partial(_hgrn_scan_kernel, layer),
        grid=(bsz, nc),
        in_specs=[fw(0), fw(1), fw(3), bw(0), bw(2), bw(3)] + [_full(x.shape) for x in consts],
        out_specs=[fw(0), bw(0)],
        out_shape=[jax.ShapeDtypeStruct((bsz, s, MIX_W), F32)] * 2,
        scratch_shapes=[pltpu.VMEM((2, HG_H, HG_D, HG_D), F32)],
        compiler_params=_params(("arbitrary", "arbitrary")),
        name="hgrn_scan",
    )(*([p_hg] * 6), *consts)


def _head_rms128(x, g):
    parts = []
    for hh in range(MIX_W // 128):
        xh = x[:, 128 * hh:128 * (hh + 1)]
        parts.append(xh * lax.rsqrt(jnp.mean(xh * xh, axis=-1, keepdims=True) + 1e-6))
    return jnp.concatenate(parts, axis=1) * g


def _gelu_tanh(x):
    cdf = 0.5 * (1.0 + jnp.tanh(math.sqrt(2.0 / math.pi) * (x + 0.044715 * (x * x * x))))
    return x * cdf


def _merge_kernel(u_ref, h_ref, ya_ref, mlf, mlb, mlo, lrf, lrb, lrg, hgf, hgb, hgg, gt_ref, mln_ref,
                  hgn_ref, wg_ref, wb_ref, wo_ref, o_ref):
    gates = _sigmoid(jnp.dot(u_ref[0], wg_ref[...], preferred_element_type=F32))
    y_b = _sigmoid(mlo[0]) * _head_rms128(mlf[0] + mlb[0], mln_ref[...])
    y_c = (lrf[0] + lrb[0]) * _gelu_tanh(lrg[0])
    gg = hgg[0]
    y_d = (gg * _sigmoid(gg)) * _head_rms128(hgf[0] + hgb[0], hgn_ref[...])
    merged = None
    for n, y in enumerate((ya_ref[0], y_b, y_c, y_d)):
        term = gates[:, D_MODEL * n:D_MODEL * (n + 1)] * _mm(y, wb_ref[n])
        merged = term if merged is None else merged + term
    o_ref[0] = h_ref[0] + gt_ref[0, 0] * _mm(merged, wo_ref[...])


def _merge(u, h, y_a, ml, p_ml, lru, p_lru, hg, p_hg, mod_l, prm):
    bsz, s, _ = h.shape
    row = lambda w, cb=0: pl.BlockSpec((1, MERGE_TILE, w), lambda b, i: (b, i, cb))
    consts = (prm["ml_norm"], prm["hg_norm"], prm["w_gate"], prm["w_branch"], prm["w_out"])
    return pl.pallas_call(
        _merge_kernel,
        grid=(bsz, s // MERGE_TILE),
        in_specs=[row(D_MODEL), row(D_MODEL), row(MIX_W),
                  row(MIX_W), row(MIX_W), row(MIX_W, 2),
                  row(MIX_W), row(MIX_W), row(MIX_W, 1),
                  row(MIX_W), row(MIX_W), row(MIX_W, 4),
                  _mod_spec(2)] + [_resident(x.shape) for x in consts],
        out_specs=row(D_MODEL),
        out_shape=jax.ShapeDtypeStruct((bsz, s, D_MODEL), F32),
        compiler_params=_params(("arbitrary", "arbitrary")),
        name="merge",
    )(u, h, y_a, ml[0], ml[1], p_ml, lru[0], lru[1], p_lru, hg[0], hg[1], p_hg, mod_l, *consts)


def _ffn_kernel(nt, final, hm, hp, hn, g_ref, sc_ref, sh_ref, gt_ref, wup_ref, cw_ref, cb_ref, wdn_ref,
                fg_ref, o_ref):
    i = pl.program_id(1)
    hx = jnp.concatenate([hp[0], hm[0], hn[0]], axis=0)
    n = hx.shape[0]
    row = _iota((n, 1), 0)
    valid = jnp.logical_and(jnp.logical_or(row >= HALO, i > 0),
                            jnp.logical_or(row < n - HALO, i < nt - 1))
    u2 = jnp.where(valid, _rmsnorm_rows(hx, g_ref[...]) * (1.0 + sc_ref[0, 0]) + sh_ref[0, 0], 0.0).astype(BF16)
    cw = cw_ref[...]
    cb = cb_ref[...]
    fc = D_FF // FF_SPLIT
    acc = None
    for j in range(FF_SPLIT):
        zs = []
        for base in (0, D_FF):
            sl = slice(base + fc * j, base + fc * (j + 1))
            y = jnp.dot(u2, wup_ref[:, sl], preferred_element_type=F32)
            z = (pltpu.roll(y, 1, 0) * cw[0:1, sl] + y * cw[1:2, sl] + pltpu.roll(y, n - 1, 0) * cw[2:3, sl]
                 + cb[:, sl])
            zs.append(z[HALO:n - HALO])
        val, gate = zs
        act = val * (gate * _sigmoid(gate))
        part = _mm(act, wdn_ref[fc * j:fc * (j + 1), :])
        acc = part if acc is None else acc + part
    out = hm[0] + gt_ref[0, 0] * acc
    if final:
        out = _rmsnorm_rows(out, fg_ref[...])
    o_ref[0] = out


def _conv_ffn(h, mod_l, prm, final_g, final):
    bsz, s, _ = h.shape
    nt = s // FFN_TILE
    consts = (prm["ffn_up"], prm["ffn_cw"], prm["ffn_cb"], prm["ffn_down"], final_g)
    row = pl.BlockSpec((1, FFN_TILE, D_MODEL), lambda b, i: (b, i, 0))
    return pl.pallas_call(
        functools.partial(_ffn_kernel, nt, final),
        grid=(bsz, nt),
        in_specs=_seq_views(D_MODEL, FFN_TILE, nt, 0, True)[:3]
        + [_full((1, D_MODEL)), _mod_spec(4), _mod_spec(3), _mod_spec(5)] + [_full(x.shape) for x in consts],
        out_specs=row,
        out_shape=jax.ShapeDtypeStruct((bsz, s, D_MODEL), F32),
        compiler_params=_params(("arbitrary", "arbitrary")),
        name="conv_ffn",
    )(h, h, h, prm["norm2_g"], mod_l, mod_l, mod_l, *consts)


def _block_diag(blocks):
    g, n, m = blocks.shape
    eye = jnp.eye(g, dtype=blocks.dtype)
    return (eye[:, None, :, None] * blocks[:, :, None, :]).reshape(g * n, g * m)


def _constants():
    idx = np.arange(CHUNK)
    tri = np.stack([idx[:, None] >= idx[None, :], idx[:, None] <= idx[None, :]]).astype(np.float32)
    hidx = np.arange(HG_CHUNK)
    sel = np.zeros((2, len(HG_LEVELS) * HG_CHUNK, HG_CHUNK), np.float32)
    for lv, half in enumerate(HG_LEVELS):
        start = (hidx // (2 * half)) * (2 * half)
        sel[0, HG_CHUNK * lv + hidx, start + half - 1] = 1.0
        sel[1, HG_CHUNK * lv + hidx, start + half] = 1.0
    lane = np.arange(MIX_W)
    seg = (lane[:, None] // RW_N == lane[None, :] // RW_N).astype(np.float32)
    ridx = np.arange(SCAN_ROWS)

    def block_tri(chunk):
        same = ridx[:, None] // chunk == ridx[None, :] // chunk
        return np.stack([same & (ridx[:, None] >= ridx[None, :]),
                         same & (ridx[:, None] <= ridx[None, :])]).astype(np.float32)

    trib, trib_big = block_tri(CHUNK), block_tri(BIG_CHUNK)
    lanes = np.stack([np.broadcast_to(np.arange(128) < 64, (CHUNK, 128)),
                      np.broadcast_to(np.arange(128) >= 64, (CHUNK, 128))]).astype(np.float32)
    return tuple(jnp.asarray(x, BF16) for x in (tri, sel, seg, trib, lanes, trib_big))


def _layer_params(l, consts, w_in, rw_mu, rw_w0, rw_w2, rw_a0, rw_a2, rw_g2, rw_kk, rw_ka, rw_rk,
                  rw_lnw, rw_lnb, ml_ibias, ml_fbias, ml_norm, lru_conv_w, lru_conv_b, lru_wa, lru_ba,
                  lru_wx, lru_bx, lru_lam, hg_norm, w_branch, w_out, norm2_g, ffn_up, ffn_conv_w,
                  ffn_conv_b, ffn_down):
    row = lambda x: x.reshape(1, -1)
    w = w_in[l]
    o_ml = RW_COLS
    o_lru = o_ml + 1552
    o_hg = o_lru + 2 * MIX_W
    o_gate = o_hg + 5 * MIX_W
    zeros = jnp.zeros((D_MODEL, 120), F32)
    w_ml = jnp.concatenate([w[:, o_ml:o_ml + 1536], w[:, o_ml + 1536:o_ml + 1544], zeros,
                            w[:, o_ml + 1544:o_ml + 1552], zeros], axis=1)
    z64 = jnp.zeros((64, MIX_W), F32)
    wlr = jnp.stack([jnp.concatenate([jnp.concatenate([rw_w2[l, d], z64], axis=1),
                                      jnp.concatenate([z64, rw_a2[l, d]], axis=1)], axis=0) for d in range(2)])
    pad = jnp.zeros((120,), F32)
    ml_bias = jnp.concatenate([ml_ibias[l].reshape(-1), pad, ml_fbias[l].reshape(-1), pad]).reshape(1, 256)
    wax = jnp.stack([jnp.concatenate([_block_diag(lru_wa[l, d]), _block_diag(lru_wx[l, d])], axis=1)
                     for d in range(2)])
    return {
        "tri": consts[0], "hg_sel": consts[1], "seg64": consts[2], "trib": consts[3], "lanes": consts[4],
        "trib_big": consts[5],
        "w_rw": w[:, :RW_COLS].astype(BF16), "w_ml": w_ml.astype(BF16),
        "w_lru": w[:, o_lru:o_hg].astype(BF16), "w_hg": w[:, o_hg:o_gate].astype(BF16),
        "w_gate": w[:, o_gate:].astype(BF16),
        "mu": row(rw_mu[l]),
        "w0a0": jnp.concatenate([rw_w0[l], rw_a0[l]], axis=1).reshape(2, 1, 2 * MIX_W),
        "wlr": wlr.astype(BF16), "kk": row(rw_kk[l]), "ka": row(rw_ka[l]), "rk": row(rw_rk[l]),
        "lnw": row(rw_lnw[l]), "lnb": row(rw_lnb[l]), "g2": rw_g2[l].astype(BF16),
        "ml_bias": ml_bias, "ml_norm": row(ml_norm[l]),
        "lru_cw": lru_conv_w[l], "lru_cb": row(lru_conv_b[l]), "lru_wax": wax.astype(BF16),
        "lru_bax": jnp.concatenate([lru_ba[l], lru_bx[l]], axis=1).reshape(2, 1, 2 * MIX_W),
        "lru_lam": lru_lam[l].reshape(2, 1, MIX_W),
        "hg_norm": row(hg_norm[l]),
        "w_branch": w_branch[l].astype(BF16), "w_out": w_out[l].astype(BF16),
        "norm2_g": row(norm2_g[l]), "ffn_up": ffn_up[l].astype(BF16), "ffn_cw": ffn_conv_w[l],
        "ffn_cb": row(ffn_conv_b[l]), "ffn_down": ffn_down[l].astype(BF16),
    }


def kernel(x, c, ada_w, ada_b, norm1_g, w_in, rw_mu, rw_w0, rw_w2, rw_a0, rw_a2, rw_g2, rw_kk, rw_ka, rw_rk, rw_lnw, rw_lnb, ml_ibias, ml_fbias, ml_norm, lru_conv_w, lru_conv_b, lru_wa, lru_ba, lru_wx, lru_bx, lru_lam, hg_lb, hg_norm, w_branch, w_out, norm2_g, ffn_up, ffn_conv_w, ffn_conv_b, ffn_down, final_g):
    depth = w_in.shape[0]
    assert x.shape[1] % LRU_CHUNK == 0 and x.shape[2] == D_MODEL
    consts = _constants()
    mod = _ada_mod(c.astype(F32), ada_w, ada_b)
    h = x.astype(F32)
    final_row = final_g.reshape(1, D_MODEL)
    hg_lb = jnp.swapaxes(hg_lb.astype(F32), 0, 1)
    for l in range(depth):
        prm = _layer_params(l, consts, w_in, rw_mu, rw_w0, rw_w2, rw_a0, rw_a2, rw_g2, rw_kk, rw_ka,
                            rw_rk, rw_lnw, rw_lnb, ml_ibias, ml_fbias, ml_norm, lru_conv_w, lru_conv_b,
                            lru_wa, lru_ba, lru_wx, lru_bx, lru_lam, hg_norm, w_branch, w_out, norm2_g,
                            ffn_up, ffn_conv_w, ffn_conv_b, ffn_down)
        mod_l = mod[l]
        u, p_rw, p_ml, p_lru, p_hg = _project(h, mod_l, norm1_g[l].reshape(1, D_MODEL), prm["w_rw"],
                                              prm["w_ml"], prm["w_lru"], prm["w_hg"])
        y_a = _rwkv_finalize(p_rw, *_rwkv_scan(p_rw, prm), prm)
        ml = _mlstm_scan(p_ml, prm)
        lru = _lru_scan(p_lru, prm)
        hg = _hgrn_scan(p_hg, hg_lb, l, prm)
        h = _merge(u, h, y_a, ml, p_ml, lru, p_lru, hg, p_hg, mod_l, prm)
        h = _conv_ffn(h, mod_l, prm, final_row, l == depth - 1)
    return h
```

```python
import functools
import math

import numpy as np
import jax
import jax.numpy as jnp
from jax import lax
from jax.experimental import pallas as pl
from jax.experimental.pallas import tpu as pltpu

F32 = jnp.float32
BF16 = jnp.bfloat16

D_MODEL = 1024
MIX_W = 512
N_BRANCH = 4
RW_N = 64
RW_H = 8
RW_GN_EPS = 64e-5
RW_COLS = 1792
ML_H = 4
ML_DV = 128
ML_DK = 64
ML_COLS_PAD = 1792
LRU_CONV = 4
RG_C = 8.0
HG_H = 4
HG_D = 128
D_FF = 2816
NEG_BIG = -1e30

CHUNK = 64
BIG_CHUNK = 128
HG_CHUNK = 64
SCAN_ROWS = 256
ROW_TILE = 256
FFN_TILE = 512
MERGE_TILE = 512
HALO = 8
FF_SPLIT = 1
VMEM_LIMIT = 56 * 1024 * 1024


def _mm(a, b):
    return jnp.dot(a.astype(BF16), b.astype(BF16), preferred_element_type=F32)


def _mm_nt(a, b):
    return lax.dot_general(a.astype(BF16), b.astype(BF16), (((1,), (1,)), ((), ())),
                           preferred_element_type=F32)


def _mm_tn(a, b):
    return lax.dot_general(a.astype(BF16), b.astype(BF16), (((0,), (0,)), ((), ())),
                           preferred_element_type=F32)


def _split3(x):
    hi = x.astype(BF16)
    r = x - hi.astype(F32)
    mid = r.astype(BF16)
    lo = (r - mid.astype(F32)).astype(BF16)
    return hi, mid, lo


def _mm_exact_l(sel, x):
    n = x.shape[1]
    r = jnp.dot(sel, jnp.concatenate(_split3(x), axis=1), preferred_element_type=F32)
    return r[:, :n] + r[:, n:2 * n] + r[:, 2 * n:]


def _sigmoid(x):
    return jax.nn.sigmoid(x)


def _softplus(x):
    return jnp.maximum(x, 0.0) + jnp.log(1.0 + jnp.exp(-jnp.abs(x)))


def _rmsnorm_rows(x, g, eps=1e-6):
    return x * lax.rsqrt(jnp.mean(x * x, axis=-1, keepdims=True) + eps) * g


def _iota(shape, dim):
    return lax.broadcasted_iota(jnp.int32, shape, dim)


def _shift_rows(x, prev_row, next_row, k):
    n = x.shape[0]
    row = _iota((n, 1), 0)
    rolled = pltpu.roll(x, (-k) % n, 0)
    if k == -1:
        return jnp.where(row == 0, prev_row, rolled)
    out = rolled
    for j in range(k):
        out = jnp.where(row == n - k + j, next_row[j:j + 1], out)
    return out


def _params(sem):
    return pltpu.CompilerParams(dimension_semantics=sem, vmem_limit_bytes=VMEM_LIMIT)


def _full(shape):
    nd = len(shape)
    return pl.BlockSpec(shape, lambda *_: (0,) * nd)


def _resident(shape):
    nd = len(shape)
    return pl.BlockSpec(shape, lambda *_: (0,) * nd, pipeline_mode=pl.Buffered(1))


def _seq_views(width, rows, nsteps, col_block, halo):
    per = rows // HALO
    last = nsteps * per - 1
    specs = []
    for rev in (False, True):
        idx = (lambda c: nsteps - 1 - c) if rev else (lambda c: c)
        specs.append(pl.BlockSpec((1, rows, width), lambda b, c, idx=idx: (b, idx(c), col_block)))
        if halo:
            specs.append(pl.BlockSpec(
                (1, HALO, width), lambda b, c, idx=idx: (b, jnp.maximum(idx(c) * per - 1, 0), col_block)))
            specs.append(pl.BlockSpec(
                (1, HALO, width), lambda b, c, idx=idx: (b, jnp.minimum((idx(c) + 1) * per, last), col_block)))
    return specs


def _tri_masks(rev):
    row = _iota((CHUNK, 128), 0)
    col = jnp.bitwise_and(_iota((CHUNK, 128), 1), CHUNK - 1)
    if rev:
        strict, incl = col > row, col >= row
    else:
        strict, incl = col < row, col <= row
    blk = jnp.right_shift(col, 4) == jnp.right_shift(row, 4)
    return strict, incl, blk, col == row


def _ada_kernel(c_ref, w_ref, b_ref, o_ref):
    c = c_ref[...]
    cond = c * _sigmoid(c)
    o_ref[0, 0] = jnp.dot(cond, w_ref[0], preferred_element_type=F32,
                          precision=lax.Precision.HIGHEST) + b_ref[0, 0]


def _ada_mod(c, ada_w, ada_b):
    depth = ada_w.shape[0]
    bsz = c.shape[0]
    out = pl.pallas_call(
        _ada_kernel,
        grid=(depth, 6),
        in_specs=[_full((bsz, D_MODEL)),
                  pl.BlockSpec((1, D_MODEL, D_MODEL), lambda l, j: (l, 0, j)),
                  pl.BlockSpec((1, 1, 1, D_MODEL), lambda l, j: (l, j, 0, 0))],
        out_specs=pl.BlockSpec((1, 1, bsz, D_MODEL), lambda l, j: (l, j, 0, 0)),
        out_shape=jax.ShapeDtypeStruct((depth, 6, bsz, D_MODEL), F32),
        compiler_params=_params(("arbitrary", "arbitrary")),
        name="ada_mod",
    )(c, ada_w, ada_b.reshape(depth, 6, 1, D_MODEL))
    return out.reshape(depth, 6, bsz, 1, D_MODEL)


def _mod_spec(idx):
    return pl.BlockSpec((1, 1, 1, D_MODEL), lambda b, i: (idx, b, 0, 0))


def _proj_kernel(h_ref, g_ref, sc_ref, sh_ref, wrw, wml, wlru, whg, u_ref, prw, pml, plru, phg):
    u = _rmsnorm_rows(h_ref[0], g_ref[...]) * (1.0 + sc_ref[0, 0]) + sh_ref[0, 0]
    ub = u.astype(BF16)
    u_ref[0] = ub
    for w_ref, o_ref in ((wrw, prw), (wml, pml), (wlru, plru), (whg, phg)):
        o_ref[0] = jnp.dot(ub, w_ref[...], preferred_element_type=F32)


def _project(h, mod_l, g1, wrw, wml, wlru, whg):
    bsz, s, _ = h.shape
    widths = [w.shape[1] for w in (wrw, wml, wlru, whg)]
    row = lambda w: pl.BlockSpec((1, ROW_TILE, w), lambda b, i: (b, i, 0))
    return pl.pallas_call(
        _proj_kernel,
        grid=(bsz, s // ROW_TILE),
        in_specs=[row(D_MODEL), _full((1, D_MODEL)), _mod_spec(1), _mod_spec(0)]
        + [_full(w.shape) for w in (wrw, wml, wlru, whg)],
        out_specs=[row(D_MODEL)] + [row(w) for w in widths],
        out_shape=[jax.ShapeDtypeStruct((bsz, s, D_MODEL), BF16)]
        + [jax.ShapeDtypeStruct((bsz, s, w), F32) for w in widths],
        compiler_params=_params(("arbitrary", "arbitrary")),
        name="in_proj",
    )(h, g1, mod_l, mod_l, wrw, wml, wlru, whg)


def _rwkv_shifted(m_ref, p_ref, n_ref, first, last, mu):
    x = m_ref[0]
    pv = jnp.where(first, 0.0, p_ref[0, HALO - 1:HALO, :])
    nv = jnp.where(last, 0.0, n_ref[0, 0:1, :])
    prev = _shift_rows(x, pv, None, -1)
    nxt = _shift_rows(x, None, nv, 1)
    return x + mu * (0.5 * (prev + nxt) - x)


def _rwkv_lowrank(xs, d, w0a0_ref, wlr_ref):
    blk = xs[:, 1536:1664]
    lane = _iota(blk.shape, 1)
    lrin = jnp.where(lane < 64, jnp.tanh(blk), blk)
    return _mm(lrin, wlr_ref[d]) + w0a0_ref[d]


def _rwkv_dir(views, first, last, rev, d, mu_ref, w0a0_ref, wlr_ref, kk_ref, ka_ref, seg_ref, tri_ref,
              lanes_ref):
    xs = _rwkv_shifted(*views, first, last, mu_ref[...])
    r, k, v = xs[:, 0:512], xs[:, 512:1024], xs[:, 1024:1536]
    lr = _rwkv_lowrank(xs, d, w0a0_ref, wlr_ref)
    lw = -jnp.exp(-_softplus(-lr[:, :MIX_W]) - 0.5)
    a = _sigmoid(lr[:, MIX_W:])
    kkm = k * kk_ref[...]
    ss = _mm(kkm * kkm, seg_ref[...])
    kkn = kkm * lax.rsqrt(jnp.maximum(ss, 1e-24))
    a_s, b_s = -kkn, kkn * a
    k_s = k * (1.0 + (a - 1.0) * ka_ref[...])

    cum = _mm_exact_l(tri_ref[d], lw)
    nchunk = xs.shape[0] // CHUNK
    tot_rows = [cum[CHUNK * i:CHUNK * i + 1] if rev else cum[CHUNK * (i + 1) - 1:CHUNK * (i + 1)]
                for i in range(nchunk)]
    ctot = jnp.concatenate([jnp.broadcast_to(t, (CHUNK, MIX_W)) for t in tot_rows], axis=0)
    e_neg = jnp.exp(-cum)
    e_hat = jnp.exp(ctot - cum)
    at, rt = a_s * jnp.exp(cum - lw), r * jnp.exp(cum)
    bt, kt = b_s * e_neg, k_s * e_neg
    bh, kh = b_s * e_hat, k_s * e_hat

    strict, incl, blk16, eye = _tri_masks(rev)
    brow = _iota((128, 128), 0) < 64
    bcol = _iota((128, 128), 1) < 64
    bdmask = brow == bcol
    masks = (strict, incl, blk16, eye, bdmask, lanes_ref[0], lanes_ref[1])
    chains = []
    for i in range(nchunk):
        rs = slice(CHUNK * i, CHUNK * (i + 1))
        gam = jnp.exp(tot_rows[i])
        chains.append([_rwkv_pair_free(at[rs, sl], rt[rs, sl], bt[rs, sl], kt[rs, sl], bh[rs, sl], kh[rs, sl],
                                       v[rs, sl], gam[:, sl], masks)
                       for sl in (slice(128 * j, 128 * (j + 1)) for j in range(RW_H // 2))])
    return chains


def _rwkv_pair_free(at, rt, bt, kt, bh, kh, vj, gam, masks):
    strict, incl, blk16, eye, bdmask, lo, hi = masks
    b16 = lambda x: x.astype(BF16)
    bdr = lambda xb: jnp.concatenate([xb * lo, xb * hi], axis=0)
    dot = lambda l, r: jnp.dot(l, r, preferred_element_type=F32)
    ar = b16(jnp.concatenate([at, rt], axis=0))
    bk = jnp.concatenate([bdr(b16(bt)), bdr(b16(kt))], axis=0)
    sc = _mm_nt(ar, bk)
    yield
    a_ab = jnp.where(strict, sc[:CHUNK, :128], 0.0)
    a_ak = jnp.where(strict, sc[:CHUNK, 128:], 0.0)
    r_b = jnp.where(incl, sc[CHUNK:, :128], 0.0)
    r_k = jnp.where(incl, sc[CHUNK:, 128:], 0.0)
    rows2 = lambda x, y: b16(jnp.concatenate([x, y], axis=0))
    a_d = jnp.where(blk16, a_ab, 0.0)
    a_o = a_ab - a_d
    a_d16, vbd = b16(a_d), bdr(b16(vj))
    a2 = dot(a_d16, bdr(a_d16))
    akrk = dot(rows2(a_ak, r_k), vbd)
    yield
    t = eye.astype(F32) + a_d
    prod = dot(rows2(t, a2), bdr(b16(a2)))
    t, a4 = t + prod[:CHUNK], prod[CHUNK:]
    yield
    prod = dot(rows2(t, a4), bdr(b16(a4)))
    t, a8 = t + prod[:CHUNK], prod[CHUNK:]
    yield
    t = t + dot(b16(t), bdr(b16(a8)))
    yield
    m = dot(b16(a_o), bdr(b16(t)))
    yield
    prod = dot(rows2(t, m), bdr(b16(m)))
    t, m2 = t + prod[:CHUNK], prod[CHUNK:]
    yield
    t = t + dot(b16(t), bdr(b16(m2)))
    yield
    return dict(ar=ar, t=b16(t), avk=akrk[:CHUNK], rkv=akrk[CHUNK:], rb=b16(r_b), vj=b16(vj),
                bhk=b16(jnp.concatenate([bh, kh], axis=0)), gam=gam, bdmask=bdmask, lo=lo, hi=hi)


def _rwkv_pair_carry(free, sbd):
    lo, hi = free["lo"], free["hi"]
    bdr = lambda xb: jnp.concatenate([xb * lo, xb * hi], axis=0)
    ars = _mm_nt(free["ar"], sbd)
    yield
    u = jnp.dot(free["t"], bdr((ars[:CHUNK] + free["avk"]).astype(BF16)), preferred_element_type=F32)
    yield
    u16 = u.astype(BF16)
    y = ars[CHUNK:] + free["rkv"] + jnp.dot(free["rb"], bdr(u16), preferred_element_type=F32)
    upd = _mm_tn(jnp.concatenate([u16, free["vj"]], axis=0), free["bhk"])
    return y, free["gam"] * sbd + jnp.where(free["bdmask"], upd, 0.0)


def _advance(chains, done):
    for i, chain in enumerate(chains):
        if i not in done:
            try:
                next(chain)
            except StopIteration as stop:
                done[i] = stop.value


def _run_round_robin(chains, riders=(), riders_done=None):
    done = {}
    while len(done) < len(chains):
        _advance(chains, done)
        _advance(riders, riders_done)
    return [done[i] for i in range(len(chains))]


def _rwkv_lru_scan_kernel(nc, fm, fp, fn, bm, bp, bn, lfm, lfp, lfn, lbm, lbp, lbn, mu_ref, w0a0_ref, wlr_ref,
                          kk_ref, ka_ref, seg_ref, tri_ref, lanes_ref, cw_ref, cb_ref, wax_ref, bax_ref, lam_ref,
                          yf_ref, yb_ref, hf_ref, hb_ref, sf_ref, sb_ref, carry_ref):
    c = pl.program_id(1)
    npair = RW_H // 2
    nchunk = SCAN_ROWS // CHUNK
    states = [[jnp.where(c == 0, 0.0, s_ref[j]) for j in range(npair)] for s_ref in (sf_ref, sb_ref)]
    common = (mu_ref, w0a0_ref, wlr_ref, kk_ref, ka_ref, seg_ref, tri_ref, lanes_ref)
    free_f = _rwkv_dir((fm, fp, fn), c == 0, c == nc - 1, False, 0, *common)
    free_b = _rwkv_dir((bm, bp, bn), c == nc - 1, c == 0, True, 1, *common)
    lru_consts = (cw_ref, cb_ref, wax_ref, bax_ref, lam_ref)
    lru = [_lru_dir((lfm, lfp, lfn), c == 0, c == nc - 1, False, 0, *lru_consts,
                    jnp.where(c == 0, 0.0, carry_ref[0])),
           _lru_dir((lbm, lbp, lbn), c == nc - 1, c == 0, True, 1, *lru_consts,
                    jnp.where(c == 0, 0.0, carry_ref[1]))]
    lru_res = {}
    res = _run_round_robin([g for chunk in free_f + free_b for g in chunk], lru, lru_res)
    free = [res[:nchunk * npair], res[nchunk * npair:]]
    y_refs = (yf_ref, yb_ref)
    outs = []
    for step in range(nchunk):
        chunk_of = (step, nchunk - 1 - step)
        res = _run_round_robin([_rwkv_pair_carry(free[dd][chunk_of[dd] * npair + j], states[dd][j])
                                for dd in range(2) for j in range(npair)], lru, lru_res)
        for dd in range(2):
            for j in range(npair):
                y, states[dd][j] = res[dd * npair + j]
                outs.append((dd, chunk_of[dd], j, y))
    while len(lru_res) < len(lru):
        _advance(lru, lru_res)
    (h_f, carry_f), (h_b, carry_b) = lru_res[0], lru_res[1]
    for dd, i, j, y in outs:
        y_refs[dd][0, CHUNK * i:CHUNK * (i + 1), 128 * j:128 * (j + 1)] = y
    for dd, s_ref in enumerate((sf_ref, sb_ref)):
        for j in range(npair):
            s_ref[j] = states[dd][j]
    hf_ref[0] = h_f
    hb_ref[0] = h_b
    carry_ref[0] = carry_f
    carry_ref[1] = carry_b


def _rwkv_lru_scan(p_rw, p_lru, prm):
    bsz, s, _ = p_rw.shape
    nc = s // SCAN_ROWS
    consts = (prm["mu"], prm["w0a0"], prm["wlr"], prm["kk"], prm["ka"], prm["seg64"], prm["trib"],
              prm["lanes"], prm["lru_cw"], prm["lru_cb"], prm["lru_wax"], prm["lru_bax"], prm["lru_lam"])
    out = lambda rev: pl.BlockSpec((1, SCAN_ROWS, MIX_W), (lambda b, c: (b, nc - 1 - c, 0)) if rev
                                   else (lambda b, c: (b, c, 0)))
    res = pl.pallas_call(
        functools.partial(_rwkv_lru_scan_kernel, nc),
        grid=(bsz, nc),
        in_specs=_seq_views(RW_COLS, SCAN_ROWS, nc, 0, True) + _seq_views(MIX_W, SCAN_ROWS, nc, 0, True)
        + [_full(x.shape) for x in consts],
        out_specs=[out(False), out(True)] * 2,
        out_shape=[jax.ShapeDtypeStruct((bsz, s, MIX_W), F32)] * 4,
        scratch_shapes=[pltpu.VMEM((RW_H // 2, 128, 128), F32)] * 2 + [pltpu.VMEM((2, 1, MIX_W), F32)],
        compiler_params=_params(("arbitrary", "arbitrary")),
        name="rwkv_lru_scan",
    )(*([p_rw] * 6), *([p_lru] * 6), *consts)
    return res[:2], res[2:]


def _rwkv_fin_kernel(nt, pm, pp, pn, yf_ref, yb_ref, mu_ref, w0a0_ref, wlr_ref, ka_ref, rk_ref, lnw_ref,
                     lnb_ref, g2_ref, seg_ref, o_ref):
    i = pl.program_id(1)
    xs = _rwkv_shifted(pm, pp, pn, i == 0, i == nt - 1, mu_ref[...])
    r, k, v = xs[:, 0:512], xs[:, 512:1024], xs[:, 1024:1536]
    kts = []
    for d in range(2):
        a = _sigmoid(_rwkv_lowrank(xs, d, w0a0_ref, wlr_ref)[:, MIX_W:])
        kts.append(k * (1.0 + (a - 1.0) * ka_ref[...]))
    y = yf_ref[0] + yb_ref[0]
    inv_n = 1.0 / RW_N
    mean = _mm(y, seg_ref[...]) * inv_n
    yc = y - mean
    var = _mm(yc * yc, seg_ref[...]) * inv_n
    y = yc * lax.rsqrt(var + RW_GN_EPS) * lnw_ref[...] + lnb_ref[...]
    k_bonus = 0.5 * (kts[0] + kts[1])
    bonus = _mm(r * k_bonus * rk_ref[...], seg_ref[...]) * v
    g = _mm(_sigmoid(xs[:, 1664:1792]), g2_ref[...])
    o_ref[0] = ((y + bonus) * g).astype(BF16)


def _rwkv_finalize(p_rw, yf, yb, prm):
    bsz, s, _ = p_rw.shape
    nt = s // ROW_TILE
    consts = (prm["mu"], prm["w0a0"], prm["wlr"], prm["ka"], prm["rk"], prm["lnw"], prm["lnb"], prm["g2"],
              prm["seg64"])
    row = pl.BlockSpec((1, ROW_TILE, MIX_W), lambda b, i: (b, i, 0))
    return pl.pallas_call(
        functools.partial(_rwkv_fin_kernel, nt),
        grid=(bsz, nt),
        in_specs=_seq_views(RW_COLS, ROW_TILE, nt, 0, True)[:3] + [row, row] + [_full(x.shape) for x in consts],
        out_specs=row,
        out_shape=jax.ShapeDtypeStruct((bsz, s, MIX_W), BF16),
        compiler_params=_params(("arbitrary", "arbitrary")),
        name="rwkv_finalize",
    )(p_rw, p_rw, p_rw, yf, yb, *consts)


def _mlstm_dir(qkv_ref, gate_ref, rev, d, bias_ref, tri_ref, c_prev, n_prev, m_prev):
    qkv = qkv_ref[0]
    nchunk = qkv.shape[0] // BIG_CHUNK
    q = qkv[:, 0:256] * (ML_DK ** -0.5)
    k = qkv[:, 256:512]
    v = qkv[:, 512:1024]
    gates = gate_ref[0] + bias_ref[...]
    g_i = gates[:, 0:128]
    g_f = -_softplus(-gates[:, 128:256])
    bcum = _mm_exact_l(tri_ref[d], g_f)
    order = range(nchunk - 1, -1, -1) if rev else range(nchunk)
    rows = [slice(BIG_CHUNK * i, BIG_CHUNK * (i + 1)) for i in range(nchunk)]

    m = m_prev
    m_before, w_c, dec = [None] * nchunk, [None] * nchunk, [None] * nchunk
    for i in order:
        g_end = (bcum[BIG_CHUNK * i:BIG_CHUNK * i + 1] if rev
                 else bcum[BIG_CHUNK * (i + 1) - 1:BIG_CHUNK * (i + 1)])
        a_end = g_end - bcum[rows[i]] + g_i[rows[i]]
        m_new = jnp.maximum(g_end + m, jnp.max(a_end, axis=0, keepdims=True))
        w_c[i] = jnp.exp(a_end - m_new)
        dec[i] = jnp.exp(g_end + m - m_new)
        m_before[i] = m
        m = m_new

    row = _iota((BIG_CHUNK, BIG_CHUNK), 0)
    col = _iota((BIG_CHUNK, BIG_CHUNK), 1)
    mask = (col >= row) if rev else (col <= row)
    lane_lo = _iota((BIG_CHUNK, 128), 1) < 64
    heads = [(hh,) + divmod(hh, 2) for hh in range(ML_H)]

    qh, qk, contrib, kw = {}, {}, {}, {}
    for i in range(nchunk):
        for hh, pair, half in heads:
            psl = slice(128 * pair, 128 * (pair + 1))
            j = 4 * d + hh
            hm = lane_lo if half == 0 else jnp.logical_not(lane_lo)
            kp = k[rows[i], psl]
            qh[i, hh] = jnp.where(hm, q[rows[i], psl], 0.0)
            kw[i, hh] = jnp.where(hm, kp, 0.0) * w_c[i][:, j:j + 1]
    for i in range(nchunk):
        for hh, pair, half in heads:
            qk[i, hh] = _mm_nt(qh[i, hh], k[rows[i], 128 * pair:128 * (pair + 1)])
            contrib[i, hh] = _mm_tn(kw[i, hh], v[rows[i], 128 * hh:128 * (hh + 1)])

    c_cur, n_cur = list(c_prev), n_prev
    c_before, n_before = {}, {}
    for i in order:
        n_before[i] = n_cur
        n_parts = []
        for pair in range(ML_H // 2):
            j0 = 4 * d + 2 * pair
            d0, d1 = dec[i][:, j0:j0 + 1], dec[i][:, j0 + 1:j0 + 2]
            c_before[i, pair] = c_cur[pair]
            c_cur[pair] = (jnp.where(_iota((128, 1), 0) < 64, d0, d1) * c_cur[pair]
                           + (contrib[i, 2 * pair] + contrib[i, 2 * pair + 1]))
            n_parts.append(jnp.where(_iota((1, 128), 1) < 64, d0, d1) * n_cur[:, 128 * pair:128 * (pair + 1)]
                           + jnp.sum(kw[i, 2 * pair] + kw[i, 2 * pair + 1], axis=0, keepdims=True))
        n_cur = jnp.concatenate(n_parts, axis=1)

    keys = [(i, hh, pair) for i in range(nchunk) for hh, pair, _ in heads]
    g_t = [(g_i[rows[i]] - bcum[rows[i]]).T for i in range(nchunk)]
    log_d, m_inter, m_t = {}, {}, {}
    for i, hh, pair in keys:
        j = 4 * d + hh
        bcol = bcum[rows[i], j:j + 1]
        log_d[i, hh] = jnp.where(mask, bcol + g_t[i][j:j + 1, :], NEG_BIG)
        m_inter[i, hh] = bcol + m_before[i][:, j:j + 1]
    for i, hh, pair in keys:
        m_t[i, hh] = jnp.maximum(m_inter[i, hh], jnp.max(log_d[i, hh], axis=1, keepdims=True))
    pw, s_int, qc, pv = {}, {}, {}, {}
    for i, hh, pair in keys:
        pw[i, hh] = jnp.where(mask, jnp.exp(log_d[i, hh] - m_t[i, hh]), 0.0) * qk[i, hh]
        s_int[i, hh] = jnp.exp(m_inter[i, hh] - m_t[i, hh])
    for i, hh, pair in keys:
        qc[i, hh] = _mm(qh[i, hh], c_before[i, pair])
        pv[i, hh] = _mm(pw[i, hh], v[rows[i], 128 * hh:128 * (hh + 1)])
    den = {}
    for i, hh, pair in keys:
        psl = slice(128 * pair, 128 * (pair + 1))
        den[i, hh] = (s_int[i, hh] * jnp.sum(qh[i, hh] * n_before[i][:, psl], axis=1, keepdims=True)
                      + jnp.sum(pw[i, hh], axis=1, keepdims=True))
    outs = [[None] * ML_H for _ in range(nchunk)]
    for i, hh, pair in keys:
        outs[i][hh] = ((s_int[i, hh] * qc[i, hh] + pv[i, hh])
                       / jnp.maximum(jnp.abs(den[i, hh]), jnp.exp(-m_t[i, hh])))
    return outs, c_cur, n_cur, m


def _mlstm_scan_kernel(fq, fg, bq, bg, bias_ref, tri_ref, hf_ref, hb_ref, c_ref, n_ref, m_ref):
    start = pl.program_id(1) == 0
    npair = ML_H // 2
    prev = [([jnp.where(start, 0.0, c_ref[d, p]) for p in range(npair)],
             jnp.where(start, 0.0, n_ref[d]), jnp.where(start, 0.0, m_ref[d])) for d in range(2)]
    res = [_mlstm_dir(fq, fg, False, 0, bias_ref, tri_ref, *prev[0]),
           _mlstm_dir(bq, bg, True, 1, bias_ref, tri_ref, *prev[1])]
    for d, h_ref in enumerate((hf_ref, hb_ref)):
        outs, c_new, n_new, m_new = res[d]
        for i, heads in enumerate(outs):
            for hh, h in enumerate(heads):
                h_ref[0, BIG_CHUNK * i:BIG_CHUNK * (i + 1), 128 * hh:128 * (hh + 1)] = h
        for p in range(npair):
            c_ref[d, p] = c_new[p]
        n_ref[d] = n_new
        m_ref[d] = m_new


def _mlstm_scan(p_ml, prm):
    bsz, s, _ = p_ml.shape
    nc = s // SCAN_ROWS
    consts = (prm["ml_bias"], prm["trib_big"])
    fw = lambda w, cb: pl.BlockSpec((1, SCAN_ROWS, w), lambda b, c: (b, c, cb))
    bw = lambda w, cb: pl.BlockSpec((1, SCAN_ROWS, w), lambda b, c: (b, nc - 1 - c, cb))
    return pl.pallas_call(
        _mlstm_scan_kernel,
        grid=(bsz, nc),
        in_specs=[fw(1024, 0), fw(256, 6), bw(1024, 0), bw(256, 6)] + [_full(x.shape) for x in consts],
        out_specs=[fw(MIX_W, 0), bw(MIX_W, 0)],
        out_shape=[jax.ShapeDtypeStruct((bsz, s, MIX_W), F32)] * 2,
        scratch_shapes=[pltpu.VMEM((2, ML_H // 2, 128, 128), F32), pltpu.VMEM((2, 1, 256), F32),
                        pltpu.VMEM((2, 1, 128), F32)],
        compiler_params=_params(("arbitrary", "arbitrary")),
        name="mlstm_scan",
    )(p_ml, p_ml, p_ml, p_ml, *consts)


def _lru_dir(views, first, last, rev, d, cw_ref, cb_ref, wax_ref, bax_ref, lam_ref, carry):
    m_ref, p_ref, n_ref = views
    x = m_ref[0]
    pv = jnp.where(first, 0.0, p_ref[0, HALO - 1:HALO, :])
    nv = jnp.where(last, 0.0, n_ref[0, 0:2, :])
    cw = cw_ref[...]
    xc = (_shift_rows(x, pv, None, -1) * cw[0:1] + x * cw[1:2] + _shift_rows(x, None, nv, 1) * cw[2:3]
          + _shift_rows(x, None, nv, 2) * cw[3:4] + cb_ref[...])
    yield
    n = x.shape[0]
    ngroups = n // 8
    row8 = _iota((1, 8, 1), 1)
    sp = RG_C * _softplus(-lam_ref[d])
    a_parts, b_parts = [], []
    for i in range(n // CHUNK):
        xi = xc[CHUNK * i:CHUNK * (i + 1)]
        rr = _mm(xi, wax_ref[d]) + bax_ref[d]
        yield
        log_a = -sp * _sigmoid(rr[:, :MIX_W])
        av = jnp.exp(log_a)
        bv = jnp.sqrt(jnp.maximum(1.0 - jnp.exp(2.0 * log_a), 0.0)) * (_sigmoid(rr[:, MIX_W:]) * xi)
        yield
        av, bv = av.reshape(CHUNK // 8, 8, MIX_W), bv.reshape(CHUNK // 8, 8, MIX_W)
        for sh in (1, 2, 4):
            shift = (8 - sh) if rev else sh
            a_sh, b_sh = pltpu.roll(av, shift, 1), pltpu.roll(bv, shift, 1)
            valid = (row8 < 8 - sh) if rev else (row8 >= sh)
            bv = jnp.where(valid, av * b_sh + bv, bv)
            av = jnp.where(valid, av * a_sh, av)
        a_parts.append(av.reshape(CHUNK, MIX_W))
        b_parts.append(bv.reshape(CHUNK, MIX_W))
        yield
    av, bv = jnp.concatenate(a_parts, axis=0), jnp.concatenate(b_parts, axis=0)
    entering = [None] * ngroups
    h = carry
    for g in (range(ngroups - 1, -1, -1) if rev else range(ngroups)):
        r = 8 * g if rev else 8 * g + 7
        entering[g] = h
        h = bv[r:r + 1] + av[r:r + 1] * h
    h_in = jnp.concatenate([jnp.broadcast_to(e, (8, MIX_W)) for e in entering], axis=0)
    return bv + av * h_in, h


HG_LEVELS = (32, 16, 8, 4, 2, 1)


def _hgrn_lower_bound(lb_ref, layer, d):
    x = lb_ref[d]
    e = jnp.exp(x - jnp.max(x, axis=0, keepdims=True))
    sm = e / jnp.sum(e, axis=0, keepdims=True)
    acc = sm[0:1]
    for j in range(1, layer + 1):
        acc = acc + sm[j:j + 1]
    return jnp.clip(acc - sm[0:1], 0.0, 1.0)


def _hgrn_dir(q_ref, f_ref, i_ref, rev, d, layer, lb_ref, tri_ref, sel_ref, states):
    lb = _hgrn_lower_bound(lb_ref, layer, d)
    qraw, fp, v = q_ref[0], f_ref[0], i_ref[0]
    nchunk = qraw.shape[0] // HG_CHUNK
    q = qraw * _sigmoid(qraw)
    logf = jnp.log(lb + (1.0 - lb) * _sigmoid(fp))
    kd = (1.0 - lb) * _sigmoid(-fp)
    gc = _mm_exact_l(tri_ref[d], logf)
    rows = [slice(HG_CHUNK * i, HG_CHUNK * (i + 1)) for i in range(nchunk)]
    tot_rows = [gc[HG_CHUNK * i:HG_CHUNK * i + 1] if rev else gc[HG_CHUNK * (i + 1) - 1:HG_CHUNK * (i + 1)]
                for i in range(nchunk)]
    gtot = jnp.concatenate([jnp.broadcast_to(t, (HG_CHUNK, MIX_W)) for t in tot_rows], axis=0)
    q_in = q * jnp.exp(gc)
    k_out = kd * jnp.exp(gtot - gc)
    gcb = gc.astype(BF16)
    ref_chunks = [jnp.dot(sel_ref[d], gcb[rows[i]], preferred_element_type=F32) for i in range(nchunk)]
    refs = [jnp.concatenate([rc[HG_CHUNK * lv:HG_CHUNK * (lv + 1)] for rc in ref_chunks], axis=0)
            for lv in range(len(HG_LEVELS))]

    row = _iota((HG_CHUNK, HG_CHUNK), 0)
    col = _iota((HG_CHUNK, HG_CHUNK), 1)
    rowc = jnp.bitwise_and(_iota((q.shape[0], 1), 0), HG_CHUNK - 1)
    qs, ks, same = [], [], []
    for lv, half in enumerate(HG_LEVELS):
        upper = lambda idx: jnp.bitwise_and(idx, 2 * half - 1) >= half
        lower = lambda idx: jnp.bitwise_and(idx, 2 * half - 1) < half
        is_q = lower(rowc) if rev else upper(rowc)
        dlt = gc - refs[lv]
        x_l = jnp.where(is_q, q, kd) * jnp.exp(jnp.where(is_q, dlt, -dlt))
        qs.append(x_l)
        ks.append(x_l)
        sh = int(math.log2(2 * half))
        q_row = lower(row) if rev else upper(row)
        k_col = upper(col) if rev else lower(col)
        same.append(jnp.logical_and(jnp.right_shift(row, sh) == jnp.right_shift(col, sh),
                                    jnp.logical_and(q_row, k_col)))

    parts, att, upd = {}, {}, {}
    for i in range(nchunk):
        for hh in range(HG_H):
            sl = slice(128 * hh, 128 * (hh + 1))
            parts[i, hh] = [_mm_nt(ql[rows[i], sl], kl[rows[i], sl]) for ql, kl in zip(qs, ks)]
            upd[i, hh] = _mm_tn(v[rows[i], sl], k_out[rows[i], sl])
    qk = q * kd
    for (i, hh), plist in parts.items():
        diag = jnp.sum(qk[rows[i], 128 * hh:128 * (hh + 1)], axis=1, keepdims=True)
        acc = jnp.where(row == col, diag, 0.0)
        for part, msk in zip(plist, same):
            acc = acc + jnp.where(msk, part, 0.0)
        att[i, hh] = acc
    cur = list(states)
    before = {}
    for i in (range(nchunk - 1, -1, -1) if rev else range(nchunk)):
        g_dec = jnp.exp(tot_rows[i])
        for hh in range(HG_H):
            before[i, hh] = cur[hh]
            cur[hh] = cur[hh] * g_dec[:, 128 * hh:128 * (hh + 1)] + upd[i, hh]
    outs = [[None] * HG_H for _ in range(nchunk)]
    for i in range(nchunk):
        for hh in range(HG_H):
            sl = slice(128 * hh, 128 * (hh + 1))
            outs[i][hh] = _mm(att[i, hh], v[rows[i], sl]) + _mm_nt(q_in[rows[i], sl], before[i, hh])
    return outs, cur


def _hgrn_scan_kernel(layer, fq, ff, fi, bq, bf, bi, lb_ref, tri_ref, sel_ref, of_ref, ob_ref, s_ref):
    start = pl.program_id(1) == 0
    prev = [[jnp.where(start, 0.0, s_ref[d, hh]) for hh in range(HG_H)] for d in range(2)]
    res = [_hgrn_dir(fq, ff, fi, False, 0, layer, lb_ref, tri_ref, sel_ref, prev[0]),
           _hgrn_dir(bq, bf, bi, True, 1, layer, lb_ref, tri_ref, sel_ref, prev[1])]
    for d, o_ref in enumerate((of_ref, ob_ref)):
        outs, new_states = res[d]
        for i, heads in enumerate(outs):
            for hh, o in enumerate(heads):
                o_ref[0, HG_CHUNK * i:HG_CHUNK * (i + 1), 128 * hh:128 * (hh + 1)] = o
        for hh in range(HG_H):
            s_ref[d, hh] = new_states[hh]


def _hgrn_scan(p_hg, hg_lb, layer, prm):
    bsz, s, _ = p_hg.shape
    nc = s // SCAN_ROWS
    consts = (hg_lb, prm["trib"], prm["hg_sel"])
    fw = lambda cb: pl.BlockSpec((1, SCAN_ROWS, MIX_W), lambda b, c: (b, c, cb))
    bw = lambda cb: pl.BlockSpec((1, SCAN_ROWS, MIX_W), lambda b, c: (b, nc - 1 - c, cb))
    return pl.pallas_call(
        functools.partial(_hgrn_scan_kernel, layer),
        grid=(bsz, nc),
        in_specs=[fw(0), fw(1), fw(3), bw(0), bw(2), bw(3)] + [_full(x.shape) for x in consts],
        out_specs=[fw(0), bw(0)],
        out_shape=[jax.ShapeDtypeStruct((bsz, s, MIX_W), F32)] * 2,
        scratch_shapes=[pltpu.VMEM((2, HG_H, HG_D, HG_D), F32)],
        compiler_params=_params(("arbitrary", "arbitrary")),
        name="hgrn_scan",
    )(*([p_hg] * 6), *consts)


def _head_rms128(x, g):
    parts = []
    for hh in range(MIX_W // 128):
        xh = x[:, 128 * hh:128 * (hh + 1)]
        parts.append(xh * lax.rsqrt(jnp.mean(xh * xh, axis=-1, keepdims=True) + 1e-6))
    return jnp.concatenate(parts, axis=1) * g


def _gelu_tanh(x):
    cdf = 0.5 * (1.0 + jnp.tanh(math.sqrt(2.0 / math.pi) * (x + 0.044715 * (x * x * x))))
    return x * cdf


def _merge_kernel(u_ref, h_ref, ya_ref, mlf, mlb, mlo, lrf, lrb, lrg, hgf, hgb, hgg, gt_ref, mln_ref,
                  hgn_ref, wg_ref, wb_ref, wo_ref, o_ref):
    gates = _sigmoid(jnp.dot(u_ref[0], wg_ref[...], preferred_element_type=F32))
    y_b = _sigmoid(mlo[0]) * _head_rms128(mlf[0] + mlb[0], mln_ref[...])
    y_c = (lrf[0] + lrb[0]) * _gelu_tanh(lrg[0])
    gg = hgg[0]
    y_d = (gg * _sigmoid(gg)) * _head_rms128(hgf[0] + hgb[0], hgn_ref[...])
    merged = None
    for n, y in enumerate((ya_ref[0], y_b, y_c, y_d)):
        term = gates[:, D_MODEL * n:D_MODEL * (n + 1)] * _mm(y, wb_ref[n])
        merged = term if merged is None else merged + term
    o_ref[0] = h_ref[0] + gt_ref[0, 0] * _mm(merged, wo_ref[...])


def _merge(u, h, y_a, ml, p_ml, lru, p_lru, hg, p_hg, mod_l, prm):
    bsz, s, _ = h.shape
    row = lambda w, cb=0: pl.BlockSpec((1, MERGE_TILE, w), lambda b, i: (b, i, cb))
    consts = (prm["ml_norm"], prm["hg_norm"], prm["w_gate"], prm["w_branch"], prm["w_out"])
    return pl.pallas_call(
        _merge_kernel,
        grid=(bsz, s // MERGE_TILE),
        in_specs=[row(D_MODEL), row(D_MODEL), row(MIX_W),
                  row(MIX_W), row(MIX_W), row(MIX_W, 2),
                  row(MIX_W), row(MIX_W), row(MIX_W, 1),
                  row(MIX_W), row(MIX_W), row(MIX_W, 4),
                  _mod_spec(2)] + [_resident(x.shape) for x in consts],
        out_specs=row(D_MODEL),
        out_shape=jax.ShapeDtypeStruct((bsz, s, D_MODEL), F32),
        compiler_params=_params(("arbitrary", "arbitrary")),
        name="merge",
    )(u, h, y_a, ml[0], ml[1], p_ml, lru[0], lru[1], p_lru, hg[0], hg[1], p_hg, mod_l, *consts)


def _ffn_kernel(nt, final, hm, hp, hn, g_ref, sc_ref, sh_ref, gt_ref, wup_ref, cw_ref, cb_ref, wdn_ref,
                fg_ref, o_ref):
    i = pl.program_id(1)
    hx = jnp.concatenate([hp[0], hm[0], hn[0]], axis=0)
    n = hx.shape[0]
    row = _iota((n, 1), 0)
    valid = jnp.logical_and(jnp.logical_or(row >= HALO, i > 0),
                            jnp.logical_or(row < n - HALO, i < nt - 1))
    u2 = jnp.where(valid, _rmsnorm_rows(hx, g_ref[...]) * (1.0 + sc_ref[0, 0]) + sh_ref[0, 0], 0.0).astype(BF16)
    cw = cw_ref[...]
    cb = cb_ref[...]
    fc = D_FF // FF_SPLIT
    acc = None
    for j in range(FF_SPLIT):
        zs = []
        for base in (0, D_FF):
            sl = slice(base + fc * j, base + fc * (j + 1))
            y = jnp.dot(u2, wup_ref[:, sl], preferred_element_type=F32)
            z = (pltpu.roll(y, 1, 0) * cw[0:1, sl] + y * cw[1:2, sl] + pltpu.roll(y, n - 1, 0) * cw[2:3, sl]
                 + cb[:, sl])
            zs.append(z[HALO:n - HALO])
        val, gate = zs
        act = val * (gate * _sigmoid(gate))
        part = _mm(act, wdn_ref[fc * j:fc * (j + 1), :])
        acc = part if acc is None else acc + part
    out = hm[0] + gt_ref[0, 0] * acc
    if final:
        out = _rmsnorm_rows(out, fg_ref[...])
    o_ref[0] = out


def _conv_ffn(h, mod_l, prm, final_g, final):
    bsz, s, _ = h.shape
    nt = s // FFN_TILE
    consts = (prm["ffn_up"], prm["ffn_cw"], prm["ffn_cb"], prm["ffn_down"], final_g)
    row = pl.BlockSpec((1, FFN_TILE, D_MODEL), lambda b, i: (b, i, 0))
    return pl.pallas_call(
        functools.partial(_ffn_kernel, nt, final),
        grid=(bsz, nt),
        in_specs=_seq_views(D_MODEL, FFN_TILE, nt, 0, True)[:3]
        + [_full((1, D_MODEL)), _mod_spec(4), _mod_spec(3), _mod_spec(5)] + [_full(x.shape) for x in consts],
        out_specs=row,
        out_shape=jax.ShapeDtypeStruct((bsz, s, D_MODEL), F32),
        compiler_params=_params(("arbitrary", "arbitrary")),
        name="conv_ffn",
    )(h, h, h, prm["norm2_g"], mod_l, mod_l, mod_l, *consts)


def _block_diag(blocks):
    g, n, m = blocks.shape
    eye = jnp.eye(g, dtype=blocks.dtype)
    return (eye[:, None, :, None] * blocks[:, :, None, :]).reshape(g * n, g * m)


def _constants():
    idx = np.arange(CHUNK)
    tri = np.stack([idx[:, None] >= idx[None, :], idx[:, None] <= idx[None, :]]).astype(np.float32)
    hidx = np.arange(HG_CHUNK)
    sel = np.zeros((2, len(HG_LEVELS) * HG_CHUNK, HG_CHUNK), np.float32)
    for lv, half in enumerate(HG_LEVELS):
        start = (hidx // (2 * half)) * (2 * half)
        sel[0, HG_CHUNK * lv + hidx, start + half - 1] = 1.0
        sel[1, HG_CHUNK * lv + hidx, start + half] = 1.0
    lane = np.arange(MIX_W)
    seg = (lane[:, None] // RW_N == lane[None, :] // RW_N).astype(np.float32)
    ridx = np.arange(SCAN_ROWS)

    def block_tri(chunk):
        same = ridx[:, None] // chunk == ridx[None, :] // chunk
        return np.stack([same & (ridx[:, None] >= ridx[None, :]),
                         same & (ridx[:, None] <= ridx[None, :])]).astype(np.float32)

    trib, trib_big = block_tri(CHUNK), block_tri(BIG_CHUNK)
    lanes = np.stack([np.broadcast_to(np.arange(128) < 64, (CHUNK, 128)),
                      np.broadcast_to(np.arange(128) >= 64, (CHUNK, 128))]).astype(np.float32)
    return tuple(jnp.asarray(x, BF16) for x in (tri, sel, seg, trib, lanes, trib_big))


def _layer_params(l, consts, w_in, rw_mu, rw_w0, rw_w2, rw_a0, rw_a2, rw_g2, rw_kk, rw_ka, rw_rk,
                  rw_lnw, rw_lnb, ml_ibias, ml_fbias, ml_norm, lru_conv_w, lru_conv_b, lru_wa, lru_ba,
                  lru_wx, lru_bx, lru_lam, hg_norm, w_branch, w_out, norm2_g, ffn_up, ffn_conv_w,
                  ffn_conv_b, ffn_down):
    row = lambda x: x.reshape(1, -1)
    w = w_in[l]
    o_ml = RW_COLS
    o_lru = o_ml + 1552
    o_hg = o_lru + 2 * MIX_W
    o_gate = o_hg + 5 * MIX_W
    zeros = jnp.zeros((D_MODEL, 120), F32)
    w_ml = jnp.concatenate([w[:, o_ml:o_ml + 1536], w[:, o_ml + 1536:o_ml + 1544], zeros,
                            w[:, o_ml + 1544:o_ml + 1552], zeros], axis=1)
    z64 = jnp.zeros((64, MIX_W), F32)
    wlr = jnp.stack([jnp.concatenate([jnp.concatenate([rw_w2[l, d], z64], axis=1),
                                      jnp.concatenate([z64, rw_a2[l, d]], axis=1)], axis=0) for d in range(2)])
    pad = jnp.zeros((120,), F32)
    ml_bias = jnp.concatenate([ml_ibias[l].reshape(-1), pad, ml_fbias[l].reshape(-1), pad]).reshape(1, 256)
    wax = jnp.stack([jnp.concatenate([_block_diag(lru_wa[l, d]), _block_diag(lru_wx[l, d])], axis=1)
                     for d in range(2)])
    return {
        "tri": consts[0], "hg_sel": consts[1], "seg64": consts[2], "trib": consts[3], "lanes": consts[4],
        "trib_big": consts[5],
        "w_rw": w[:, :RW_COLS].astype(BF16), "w_ml": w_ml.astype(BF16),
        "w_lru": w[:, o_lru:o_hg].astype(BF16), "w_hg": w[:, o_hg:o_gate].astype(BF16),
        "w_gate": w[:, o_gate:].astype(BF16),
        "mu": row(rw_mu[l]),
        "w0a0": jnp.concatenate([rw_w0[l], rw_a0[l]], axis=1).reshape(2, 1, 2 * MIX_W),
        "wlr": wlr.astype(BF16), "kk": row(rw_kk[l]), "ka": row(rw_ka[l]), "rk": row(rw_rk[l]),
        "lnw": row(rw_lnw[l]), "lnb": row(rw_lnb[l]), "g2": rw_g2[l].astype(BF16),
        "ml_bias": ml_bias, "ml_norm": row(ml_norm[l]),
        "lru_cw": lru_conv_w[l], "lru_cb": row(lru_conv_b[l]), "lru_wax": wax.astype(BF16),
        "lru_bax": jnp.concatenate([lru_ba[l], lru_bx[l]], axis=1).reshape(2, 1, 2 * MIX_W),
        "lru_lam": lru_lam[l].reshape(2, 1, MIX_W),
        "hg_norm": row(hg_norm[l]),
        "w_branch": w_branch[l].astype(BF16), "w_out": w_out[l].astype(BF16),
        "norm2_g": row(norm2_g[l]), "ffn_up": ffn_up[l].astype(BF16), "ffn_cw": ffn_conv_w[l],
        "ffn_cb": row(ffn_conv_b[l]), "ffn_down": ffn_down[l].astype(BF16),
    }


def kernel(x, c, ada_w, ada_b, norm1_g, w_in, rw_mu, rw_w0, rw_w2, rw_a0, rw_a2, rw_g2, rw_kk, rw_ka, rw_rk, rw_lnw, rw_lnb, ml_ibias, ml_fbias, ml_norm, lru_conv_w, lru_conv_b, lru_wa, lru_ba, lru_wx, lru_bx, lru_lam, hg_lb, hg_norm, w_branch, w_out, norm2_g, ffn_up, ffn_conv_w, ffn_conv_b, ffn_down, final_g):
    depth = w_in.shape[0]
    assert x.shape[1] % MERGE_TILE == 0 and x.shape[2] == D_MODEL
    consts = _constants()
    mod = _ada_mod(c.astype(F32), ada_w, ada_b)
    h = x.astype(F32)
    final_row = final_g.reshape(1, D_MODEL)
    hg_lb = jnp.swapaxes(hg_lb.astype(F32), 0, 1)
    for l in range(depth):
        prm = _layer_params(l, consts, w_in, rw_mu, rw_w0, rw_w2, rw_a0, rw_a2, rw_g2, rw_kk, rw_ka,
                            rw_rk, rw_lnw, rw_lnb, ml_ibias, ml_fbias, ml_norm, lru_conv_w, lru_conv_b,
                            lru_wa, lru_ba, lru_wx, lru_bx, lru_lam, hg_norm, w_branch, w_out, norm2_g,
                            ffn_up, ffn_conv_w, ffn_conv_b, ffn_down)
        mod_l = mod[l]
        u, p_rw, p_ml, p_lru, p_hg = _project(h, mod_l, norm1_g[l].reshape(1, D_MODEL), prm["w_rw"],
                                              prm["w_ml"], prm["w_lru"], prm["w_hg"])
        rw, lru = _rwkv_lru_scan(p_rw, p_lru, prm)
        y_a = _rwkv_finalize(p_rw, *rw, prm)
        ml = _mlstm_scan(p_ml, prm)
        hg = _hgrn_scan(p_hg, hg_lb, l, prm)
        h = _merge(u, h, y_a, ml, p_ml, lru, p_lru, hg, p_hg, mod_l, prm)
        h = _conv_ffn(h, mod_l, prm, final_row, l == depth - 1)
    return h
```

```python
import functools
import math

import numpy as np
import jax
import jax.numpy as jnp
from jax import lax
from jax.experimental import pallas as pl
from jax.experimental.pallas import tpu as pltpu

F32 = jnp.float32
BF16 = jnp.bfloat16

D_MODEL = 1024
MIX_W = 512
N_BRANCH = 4
RW_N = 64
RW_H = 8
RW_GN_EPS = 64e-5
RW_COLS = 1792
ML_H = 4
ML_DV = 128
ML_DK = 64
ML_COLS_PAD = 1792
LRU_CONV = 4
RG_C = 8.0
HG_H = 4
HG_D = 128
D_FF = 2816
NEG_BIG = -1e30

CHUNK = 64
BIG_CHUNK = 128
HG_CHUNK = 64
SCAN_ROWS = 256
ROW_TILE = 256
FFN_TILE = 512
MERGE_TILE = 512
HALO = 8
FF_SPLIT = 1
VMEM_LIMIT = 56 * 1024 * 1024


def _mm(a, b):
    return jnp.dot(a.astype(BF16), b.astype(BF16), preferred_element_type=F32)


def _mm_nt(a, b):
    return lax.dot_general(a.astype(BF16), b.astype(BF16), (((1,), (1,)), ((), ())),
                           preferred_element_type=F32)


def _mm_tn(a, b):
    return lax.dot_general(a.astype(BF16), b.astype(BF16), (((0,), (0,)), ((), ())),
                           preferred_element_type=F32)


def _split3(x):
    hi = x.astype(BF16)
    r = x - hi.astype(F32)
    mid = r.astype(BF16)
    lo = (r - mid.astype(F32)).astype(BF16)
    return hi, mid, lo


def _mm_exact_l(sel, x):
    n = x.shape[1]
    r = jnp.dot(sel, jnp.concatenate(_split3(x), axis=1), preferred_element_type=F32)
    return r[:, :n] + r[:, n:2 * n] + r[:, 2 * n:]


def _sigmoid(x):
    return jax.nn.sigmoid(x)


def _softplus(x):
    return jnp.maximum(x, 0.0) + jnp.log(1.0 + jnp.exp(-jnp.abs(x)))


def _rmsnorm_rows(x, g, eps=1e-6):
    return x * lax.rsqrt(jnp.mean(x * x, axis=-1, keepdims=True) + eps) * g


def _iota(shape, dim):
    return lax.broadcasted_iota(jnp.int32, shape, dim)


def _shift_rows(x, prev_row, next_row, k):
    n = x.shape[0]
    row = _iota((n, 1), 0)
    rolled = pltpu.roll(x, (-k) % n, 0)
    if k == -1:
        return jnp.where(row == 0, prev_row, rolled)
    out = rolled
    for j in range(k):
        out = jnp.where(row == n - k + j, next_row[j:j + 1], out)
    return out


def _params(sem):
    return pltpu.CompilerParams(dimension_semantics=sem, vmem_limit_bytes=VMEM_LIMIT)


def _full(shape):
    nd = len(shape)
    return pl.BlockSpec(shape, lambda *_: (0,) * nd)


def _resident(shape):
    nd = len(shape)
    return pl.BlockSpec(shape, lambda *_: (0,) * nd, pipeline_mode=pl.Buffered(1))


def _seq_views(width, rows, nsteps, col_block, halo):
    per = rows // HALO
    last = nsteps * per - 1
    specs = []
    for rev in (False, True):
        idx = (lambda c: nsteps - 1 - c) if rev else (lambda c: c)
        specs.append(pl.BlockSpec((1, rows, width), lambda b, c, idx=idx: (b, idx(c), col_block)))
        if halo:
            specs.append(pl.BlockSpec(
                (1, HALO, width), lambda b, c, idx=idx: (b, jnp.maximum(idx(c) * per - 1, 0), col_block)))
            specs.append(pl.BlockSpec(
                (1, HALO, width), lambda b, c, idx=idx: (b, jnp.minimum((idx(c) + 1) * per, last), col_block)))
    return specs


def _tri_masks(rev):
    row = _iota((CHUNK, 128), 0)
    col = jnp.bitwise_and(_iota((CHUNK, 128), 1), CHUNK - 1)
    if rev:
        strict, incl = col > row, col >= row
    else:
        strict, incl = col < row, col <= row
    blk = jnp.right_shift(col, 4) == jnp.right_shift(row, 4)
    return strict, incl, blk, col == row


def _ada_kernel(c_ref, w_ref, b_ref, o_ref):
    c = c_ref[...]
    cond = c * _sigmoid(c)
    o_ref[0, 0] = jnp.dot(cond, w_ref[0], preferred_element_type=F32,
                          precision=lax.Precision.HIGHEST) + b_ref[0, 0]


def _ada_mod(c, ada_w, ada_b):
    depth = ada_w.shape[0]
    bsz = c.shape[0]
    out = pl.pallas_call(
        _ada_kernel,
        grid=(depth, 6),
        in_specs=[_full((bsz, D_MODEL)),
                  pl.BlockSpec((1, D_MODEL, D_MODEL), lambda l, j: (l, 0, j)),
                  pl.BlockSpec((1, 1, 1, D_MODEL), lambda l, j: (l, j, 0, 0))],
        out_specs=pl.BlockSpec((1, 1, bsz, D_MODEL), lambda l, j: (l, j, 0, 0)),
        out_shape=jax.ShapeDtypeStruct((depth, 6, bsz, D_MODEL), F32),
        compiler_params=_params(("arbitrary", "arbitrary")),
        name="ada_mod",
    )(c, ada_w, ada_b.reshape(depth, 6, 1, D_MODEL))
    return out.reshape(depth, 6, bsz, 1, D_MODEL)


def _mod_spec(idx):
    return pl.BlockSpec((1, 1, 1, D_MODEL), lambda b, i: (idx, b, 0, 0))


def _proj_kernel(h_ref, g_ref, sc_ref, sh_ref, wrw, wml, wlru, whg, u_ref, prw, pml, plru, phg):
    u = _rmsnorm_rows(h_ref[0], g_ref[...]) * (1.0 + sc_ref[0, 0]) + sh_ref[0, 0]
    ub = u.astype(BF16)
    u_ref[0] = ub
    for w_ref, o_ref in ((wrw, prw), (wml, pml), (wlru, plru), (whg, phg)):
        o_ref[0] = jnp.dot(ub, w_ref[...], preferred_element_type=F32)


def _project(h, mod_l, g1, wrw, wml, wlru, whg):
    bsz, s, _ = h.shape
    widths = [w.shape[1] for w in (wrw, wml, wlru, whg)]
    row = lambda w: pl.BlockSpec((1, ROW_TILE, w), lambda b, i: (b, i, 0))
    return pl.pallas_call(
        _proj_kernel,
        grid=(bsz, s // ROW_TILE),
        in_specs=[row(D_MODEL), _full((1, D_MODEL)), _mod_spec(1), _mod_spec(0)]
        + [_full(w.shape) for w in (wrw, wml, wlru, whg)],
        out_specs=[row(D_MODEL)] + [row(w) for w in widths],
        out_shape=[jax.ShapeDtypeStruct((bsz, s, D_MODEL), BF16)]
        + [jax.ShapeDtypeStruct((bsz, s, w), F32) for w in widths],
        compiler_params=_params(("arbitrary", "arbitrary")),
        name="in_proj",
    )(h, g1, mod_l, mod_l, wrw, wml, wlru, whg)


def _rwkv_shifted(m_ref, p_ref, n_ref, first, last, mu):
    x = m_ref[0]
    pv = jnp.where(first, 0.0, p_ref[0, HALO - 1:HALO, :])
    nv = jnp.where(last, 0.0, n_ref[0, 0:1, :])
    prev = _shift_rows(x, pv, None, -1)
    nxt = _shift_rows(x, None, nv, 1)
    return x + mu * (0.5 * (prev + nxt) - x)


def _rwkv_lowrank(xs, d, w0a0_ref, wlr_ref):
    blk = xs[:, 1536:1664]
    lane = _iota(blk.shape, 1)
    lrin = jnp.where(lane < 64, jnp.tanh(blk), blk)
    return _mm(lrin, wlr_ref[d]) + w0a0_ref[d]


def _rwkv_dir(views, first, last, rev, d, mu_ref, w0a0_ref, wlr_ref, kk_ref, ka_ref, seg_ref, tri_ref,
              lanes_ref):
    xs = _rwkv_shifted(*views, first, last, mu_ref[...])
    r, k, v = xs[:, 0:512], xs[:, 512:1024], xs[:, 1024:1536]
    lr = _rwkv_lowrank(xs, d, w0a0_ref, wlr_ref)
    lw = -jnp.exp(-_softplus(-lr[:, :MIX_W]) - 0.5)
    a = _sigmoid(lr[:, MIX_W:])
    kkm = k * kk_ref[...]
    ss = _mm(kkm * kkm, seg_ref[...])
    kkn = kkm * lax.rsqrt(jnp.maximum(ss, 1e-24))
    a_s, b_s = -kkn, kkn * a
    k_s = k * (1.0 + (a - 1.0) * ka_ref[...])

    cum = _mm_exact_l(tri_ref[d], lw)
    nchunk = xs.shape[0] // CHUNK
    tot_rows = [cum[CHUNK * i:CHUNK * i + 1] if rev else cum[CHUNK * (i + 1) - 1:CHUNK * (i + 1)]
                for i in range(nchunk)]
    ctot = jnp.concatenate([jnp.broadcast_to(t, (CHUNK, MIX_W)) for t in tot_rows], axis=0)
    e_neg = jnp.exp(-cum)
    e_hat = jnp.exp(ctot - cum)
    at, rt = a_s * jnp.exp(cum - lw), r * jnp.exp(cum)
    bt, kt = b_s * e_neg, k_s * e_neg
    bh, kh = b_s * e_hat, k_s * e_hat

    strict, incl, blk16, eye = _tri_masks(rev)
    brow = _iota((128, 128), 0) < 64
    bcol = _iota((128, 128), 1) < 64
    bdmask = brow == bcol
    masks = (strict, incl, blk16, eye, bdmask, lanes_ref[0], lanes_ref[1])
    chains = []
    for i in range(nchunk):
        rs = slice(CHUNK * i, CHUNK * (i + 1))
        gam = jnp.exp(tot_rows[i])
        chains.append([_rwkv_pair_free(at[rs, sl], rt[rs, sl], bt[rs, sl], kt[rs, sl], bh[rs, sl], kh[rs, sl],
                                       v[rs, sl], gam[:, sl], masks)
                       for sl in (slice(128 * j, 128 * (j + 1)) for j in range(RW_H // 2))])
    return chains


def _rwkv_pair_free(at, rt, bt, kt, bh, kh, vj, gam, masks):
    strict, incl, blk16, eye, bdmask, lo, hi = masks
    b16 = lambda x: x.astype(BF16)
    bdr = lambda xb: jnp.concatenate([xb * lo, xb * hi], axis=0)
    dot = lambda l, r: jnp.dot(l, r, preferred_element_type=F32)
    ar = b16(jnp.concatenate([at, rt], axis=0))
    bk = jnp.concatenate([bdr(b16(bt)), bdr(b16(kt))], axis=0)
    sc = _mm_nt(ar, bk)
    yield
    a_ab = jnp.where(strict, sc[:CHUNK, :128], 0.0)
    a_ak = jnp.where(strict, sc[:CHUNK, 128:], 0.0)
    r_b = jnp.where(incl, sc[CHUNK:, :128], 0.0)
    r_k = jnp.where(incl, sc[CHUNK:, 128:], 0.0)
    rows2 = lambda x, y: b16(jnp.concatenate([x, y], axis=0))
    a_d = jnp.where(blk16, a_ab, 0.0)
    a_o = a_ab - a_d
    a_d16, vbd = b16(a_d), bdr(b16(vj))
    a2 = dot(a_d16, bdr(a_d16))
    akrk = dot(rows2(a_ak, r_k), vbd)
    yield
    t = eye.astype(F32) + a_d
    prod = dot(rows2(t, a2), bdr(b16(a2)))
    t, a4 = t + prod[:CHUNK], prod[CHUNK:]
    yield
    prod = dot(rows2(t, a4), bdr(b16(a4)))
    t, a8 = t + prod[:CHUNK], prod[CHUNK:]
    yield
    t = t + dot(b16(t), bdr(b16(a8)))
    yield
    m = dot(b16(a_o), bdr(b16(t)))
    yield
    prod = dot(rows2(t, m), bdr(b16(m)))
    t, m2 = t + prod[:CHUNK], prod[CHUNK:]
    yield
    t = t + dot(b16(t), bdr(b16(m2)))
    yield
    return dict(ar=ar, t=b16(t), avk=akrk[:CHUNK], rkv=akrk[CHUNK:], rb=b16(r_b), vj=b16(vj),
                bhk=b16(jnp.concatenate([bh, kh], axis=0)), gam=gam, bdmask=bdmask, lo=lo, hi=hi)


def _rwkv_pair_carry(free, sbd):
    lo, hi = free["lo"], free["hi"]
    bdr = lambda xb: jnp.concatenate([xb * lo, xb * hi], axis=0)
    ars = _mm_nt(free["ar"], sbd)
    yield
    u = jnp.dot(free["t"], bdr((ars[:CHUNK] + free["avk"]).astype(BF16)), preferred_element_type=F32)
    yield
    u16 = u.astype(BF16)
    y = ars[CHUNK:] + free["rkv"] + jnp.dot(free["rb"], bdr(u16), preferred_element_type=F32)
    upd = _mm_tn(jnp.concatenate([u16, free["vj"]], axis=0), free["bhk"])
    return y, free["gam"] * sbd + jnp.where(free["bdmask"], upd, 0.0)


def _advance(chains, done):
    for i, chain in enumerate(chains):
        if i not in done:
            try:
                next(chain)
            except StopIteration as stop:
                done[i] = stop.value


def _run_round_robin(chains, riders=(), riders_done=None):
    done = {}
    while len(done) < len(chains):
        _advance(chains, done)
        _advance(riders, riders_done)
    return [done[i] for i in range(len(chains))]


def _rwkv_lru_scan_kernel(nc, fm, fp, fn, bm, bp, bn, lfm, lfp, lfn, lbm, lbp, lbn, mu_ref, w0a0_ref, wlr_ref,
                          kk_ref, ka_ref, seg_ref, tri_ref, lanes_ref, cw_ref, cb_ref, wax_ref, bax_ref, lam_ref,
                          yf_ref, yb_ref, hf_ref, hb_ref, sf_ref, sb_ref, carry_ref):
    c = pl.program_id(1)
    npair = RW_H // 2
    nchunk = SCAN_ROWS // CHUNK
    states = [[jnp.where(c == 0, 0.0, s_ref[j]) for j in range(npair)] for s_ref in (sf_ref, sb_ref)]
    common = (mu_ref, w0a0_ref, wlr_ref, kk_ref, ka_ref, seg_ref, tri_ref, lanes_ref)
    free_f = _rwkv_dir((fm, fp, fn), c == 0, c == nc - 1, False, 0, *common)
    free_b = _rwkv_dir((bm, bp, bn), c == nc - 1, c == 0, True, 1, *common)
    lru_consts = (cw_ref, cb_ref, wax_ref, bax_ref, lam_ref)
    lru = [_lru_dir((lfm, lfp, lfn), c == 0, c == nc - 1, False, 0, *lru_consts,
                    jnp.where(c == 0, 0.0, carry_ref[0])),
           _lru_dir((lbm, lbp, lbn), c == nc - 1, c == 0, True, 1, *lru_consts,
                    jnp.where(c == 0, 0.0, carry_ref[1]))]
    lru_res = {}
    res = _run_round_robin([g for chunk in free_f + free_b for g in chunk], lru, lru_res)
    free = [res[:nchunk * npair], res[nchunk * npair:]]
    y_refs = (yf_ref, yb_ref)
    outs = []
    for step in range(nchunk):
        chunk_of = (step, nchunk - 1 - step)
        res = _run_round_robin([_rwkv_pair_carry(free[dd][chunk_of[dd] * npair + j], states[dd][j])
                                for dd in range(2) for j in range(npair)], lru, lru_res)
        for dd in range(2):
            for j in range(npair):
                y, states[dd][j] = res[dd * npair + j]
                outs.append((dd, chunk_of[dd], j, y))
    while len(lru_res) < len(lru):
        _advance(lru, lru_res)
    (h_f, carry_f), (h_b, carry_b) = lru_res[0], lru_res[1]
    for dd, i, j, y in outs:
        y_refs[dd][0, CHUNK * i:CHUNK * (i + 1), 128 * j:128 * (j + 1)] = y
    for dd, s_ref in enumerate((sf_ref, sb_ref)):
        for j in range(npair):
            s_ref[j] = states[dd][j]
    hf_ref[0] = h_f
    hb_ref[0] = h_b
    carry_ref[0] = carry_f
    carry_ref[1] = carry_b


def _rwkv_lru_scan(p_rw, p_lru, prm):
    bsz, s, _ = p_rw.shape
    nc = s // SCAN_ROWS
    consts = (prm["mu"], prm["w0a0"], prm["wlr"], prm["kk"], prm["ka"], prm["seg64"], prm["trib"],
              prm["lanes"], prm["lru_cw"], prm["lru_cb"], prm["lru_wax"], prm["lru_bax"], prm["lru_lam"])
    out = lambda rev: pl.BlockSpec((1, SCAN_ROWS, MIX_W), (lambda b, c: (b, nc - 1 - c, 0)) if rev
                                   else (lambda b, c: (b, c, 0)))
    res = pl.pallas_call(
        functools.partial(_rwkv_lru_scan_kernel, nc),
        grid=(bsz, nc),
        in_specs=_seq_views(RW_COLS, SCAN_ROWS, nc, 0, True) + _seq_views(MIX_W, SCAN_ROWS, nc, 0, True)
        + [_full(x.shape) for x in consts],
        out_specs=[out(False), out(True)] * 2,
        out_shape=[jax.ShapeDtypeStruct((bsz, s, MIX_W), F32)] * 4,
        scratch_shapes=[pltpu.VMEM((RW_H // 2, 128, 128), F32)] * 2 + [pltpu.VMEM((2, 1, MIX_W), F32)],
        compiler_params=_params(("arbitrary", "arbitrary")),
        name="rwkv_lru_scan",
    )(*([p_rw] * 6), *([p_lru] * 6), *consts)
    return res[:2], res[2:]


def _rwkv_fin_chain(first, last, pm, pp, pn, yf_ref, yb_ref, mu_ref, w0a0_ref, wlr_ref, ka_ref, rk_ref, lnw_ref,
                    lnb_ref, g2_ref, seg_ref):
    xs = _rwkv_shifted(pm, pp, pn, first, last, mu_ref[...])
    r, k, v = xs[:, 0:512], xs[:, 512:1024], xs[:, 1024:1536]
    yield
    kts = []
    for d in range(2):
        a = _sigmoid(_rwkv_lowrank(xs, d, w0a0_ref, wlr_ref)[:, MIX_W:])
        kts.append(k * (1.0 + (a - 1.0) * ka_ref[...]))
    yield
    y = yf_ref[0] + yb_ref[0]
    inv_n = 1.0 / RW_N
    mean = _mm(y, seg_ref[...]) * inv_n
    yield
    yc = y - mean
    var = _mm(yc * yc, seg_ref[...]) * inv_n
    yield
    y = yc * lax.rsqrt(var + RW_GN_EPS) * lnw_ref[...] + lnb_ref[...]
    k_bonus = 0.5 * (kts[0] + kts[1])
    bonus = _mm(r * k_bonus * rk_ref[...], seg_ref[...]) * v
    g = _mm(_sigmoid(xs[:, 1664:1792]), g2_ref[...])
    yield
    return ((y + bonus) * g).astype(BF16)


def _mlstm_dir(qkv_ref, gate_ref, rev, d, bias_ref, tri_ref, c_prev, n_prev, m_prev):
    qkv = qkv_ref[0]
    nchunk = qkv.shape[0] // BIG_CHUNK
    q = qkv[:, 0:256] * (ML_DK ** -0.5)
    k = qkv[:, 256:512]
    v = qkv[:, 512:1024]
    gates = gate_ref[0] + bias_ref[...]
    g_i = gates[:, 0:128]
    g_f = -_softplus(-gates[:, 128:256])
    bcum = _mm_exact_l(tri_ref[d], g_f)
    order = range(nchunk - 1, -1, -1) if rev else range(nchunk)
    rows = [slice(BIG_CHUNK * i, BIG_CHUNK * (i + 1)) for i in range(nchunk)]

    m = m_prev
    m_before, w_c, dec = [None] * nchunk, [None] * nchunk, [None] * nchunk
    for i in order:
        g_end = (bcum[BIG_CHUNK * i:BIG_CHUNK * i + 1] if rev
                 else bcum[BIG_CHUNK * (i + 1) - 1:BIG_CHUNK * (i + 1)])
        a_end = g_end - bcum[rows[i]] + g_i[rows[i]]
        m_new = jnp.maximum(g_end + m, jnp.max(a_end, axis=0, keepdims=True))
        w_c[i] = jnp.exp(a_end - m_new)
        dec[i] = jnp.exp(g_end + m - m_new)
        m_before[i] = m
        m = m_new

    row = _iota((BIG_CHUNK, BIG_CHUNK), 0)
    col = _iota((BIG_CHUNK, BIG_CHUNK), 1)
    mask = (col >= row) if rev else (col <= row)
    lane_lo = _iota((BIG_CHUNK, 128), 1) < 64
    heads = [(hh,) + divmod(hh, 2) for hh in range(ML_H)]

    yield
    qh, qk, contrib, kw = {}, {}, {}, {}
    for i in range(nchunk):
        for hh, pair, half in heads:
            psl = slice(128 * pair, 128 * (pair + 1))
            j = 4 * d + hh
            hm = lane_lo if half == 0 else jnp.logical_not(lane_lo)
            kp = k[rows[i], psl]
            qh[i, hh] = jnp.where(hm, q[rows[i], psl], 0.0)
            kw[i, hh] = jnp.where(hm, kp, 0.0) * w_c[i][:, j:j + 1]
    for i in range(nchunk):
        for hh, pair, half in heads:
            qk[i, hh] = _mm_nt(qh[i, hh], k[rows[i], 128 * pair:128 * (pair + 1)])
            contrib[i, hh] = _mm_tn(kw[i, hh], v[rows[i], 128 * hh:128 * (hh + 1)])

    yield
    c_cur, n_cur = list(c_prev), n_prev
    c_before, n_before = {}, {}
    for i in order:
        n_before[i] = n_cur
        n_parts = []
        for pair in range(ML_H // 2):
            j0 = 4 * d + 2 * pair
            d0, d1 = dec[i][:, j0:j0 + 1], dec[i][:, j0 + 1:j0 + 2]
            c_before[i, pair] = c_cur[pair]
            c_cur[pair] = (jnp.where(_iota((128, 1), 0) < 64, d0, d1) * c_cur[pair]
                           + (contrib[i, 2 * pair] + contrib[i, 2 * pair + 1]))
            n_parts.append(jnp.where(_iota((1, 128), 1) < 64, d0, d1) * n_cur[:, 128 * pair:128 * (pair + 1)]
                           + jnp.sum(kw[i, 2 * pair] + kw[i, 2 * pair + 1], axis=0, keepdims=True))
        n_cur = jnp.concatenate(n_parts, axis=1)

    keys = [(i, hh, pair) for i in range(nchunk) for hh, pair, _ in heads]
    g_t = [(g_i[rows[i]] - bcum[rows[i]]).T for i in range(nchunk)]
    log_d, m_inter, m_t = {}, {}, {}
    for i, hh, pair in keys:
        j = 4 * d + hh
        bcol = bcum[rows[i], j:j + 1]
        log_d[i, hh] = jnp.where(mask, bcol + g_t[i][j:j + 1, :], NEG_BIG)
        m_inter[i, hh] = bcol + m_before[i][:, j:j + 1]
    for i, hh, pair in keys:
        m_t[i, hh] = jnp.maximum(m_inter[i, hh], jnp.max(log_d[i, hh], axis=1, keepdims=True))
    yield
    pw, s_int, qc, pv = {}, {}, {}, {}
    for i, hh, pair in keys:
        pw[i, hh] = jnp.where(mask, jnp.exp(log_d[i, hh] - m_t[i, hh]), 0.0) * qk[i, hh]
        s_int[i, hh] = jnp.exp(m_inter[i, hh] - m_t[i, hh])
    for i, hh, pair in keys:
        qc[i, hh] = _mm(qh[i, hh], c_before[i, pair])
        pv[i, hh] = _mm(pw[i, hh], v[rows[i], 128 * hh:128 * (hh + 1)])
    yield
    den = {}
    for i, hh, pair in keys:
        psl = slice(128 * pair, 128 * (pair + 1))
        den[i, hh] = (s_int[i, hh] * jnp.sum(qh[i, hh] * n_before[i][:, psl], axis=1, keepdims=True)
                      + jnp.sum(pw[i, hh], axis=1, keepdims=True))
    outs = [[None] * ML_H for _ in range(nchunk)]
    for i, hh, pair in keys:
        outs[i][hh] = ((s_int[i, hh] * qc[i, hh] + pv[i, hh])
                       / jnp.maximum(jnp.abs(den[i, hh]), jnp.exp(-m_t[i, hh])))
    return outs, c_cur, n_cur, m


def _mlstm_scan_kernel(nc, fq, fg, bq, bg, pm, pp, pn, yf_ref, yb_ref, bias_ref, tri_ref, mu_ref, w0a0_ref, wlr_ref,
                       ka_ref, rk_ref, lnw_ref, lnb_ref, g2_ref, seg_ref, hf_ref, hb_ref, ya_ref, c_ref, n_ref, m_ref):
    step = pl.program_id(1)
    start = step == 0
    npair = ML_H // 2
    prev = [([jnp.where(start, 0.0, c_ref[d, p]) for p in range(npair)],
             jnp.where(start, 0.0, n_ref[d]), jnp.where(start, 0.0, m_ref[d])) for d in range(2)]
    fin = _rwkv_fin_chain(start, step == nc - 1, pm, pp, pn, yf_ref, yb_ref, mu_ref, w0a0_ref, wlr_ref, ka_ref,
                          rk_ref, lnw_ref, lnb_ref, g2_ref, seg_ref)
    res = _run_round_robin([_mlstm_dir(fq, fg, False, 0, bias_ref, tri_ref, *prev[0]),
                            _mlstm_dir(bq, bg, True, 1, bias_ref, tri_ref, *prev[1]), fin])
    ya_ref[0] = res[2]
    for d, h_ref in enumerate((hf_ref, hb_ref)):
        outs, c_new, n_new, m_new = res[d]
        for i, heads in enumerate(outs):
            for hh, h in enumerate(heads):
                h_ref[0, BIG_CHUNK * i:BIG_CHUNK * (i + 1), 128 * hh:128 * (hh + 1)] = h
        for p in range(npair):
            c_ref[d, p] = c_new[p]
        n_ref[d] = n_new
        m_ref[d] = m_new


def _mlstm_scan(p_ml, p_rw, yf, yb, prm):
    bsz, s, _ = p_ml.shape
    nc = s // SCAN_ROWS
    consts = (prm["ml_bias"], prm["trib_big"], prm["mu"], prm["w0a0"], prm["wlr"], prm["ka"], prm["rk"],
              prm["lnw"], prm["lnb"], prm["g2"], prm["seg64"])
    fw = lambda w, cb: pl.BlockSpec((1, SCAN_ROWS, w), lambda b, c: (b, c, cb))
    bw = lambda w, cb: pl.BlockSpec((1, SCAN_ROWS, w), lambda b, c: (b, nc - 1 - c, cb))
    hf, hb, y_a = pl.pallas_call(
        functools.partial(_mlstm_scan_kernel, nc),
        grid=(bsz, nc),
        in_specs=[fw(1024, 0), fw(256, 6), bw(1024, 0), bw(256, 6)]
        + _seq_views(RW_COLS, SCAN_ROWS, nc, 0, True)[:3] + [fw(MIX_W, 0), fw(MIX_W, 0)]
        + [_full(x.shape) for x in consts],
        out_specs=[fw(MIX_W, 0), bw(MIX_W, 0), fw(MIX_W, 0)],
        out_shape=[jax.ShapeDtypeStruct((bsz, s, MIX_W), F32)] * 2 + [jax.ShapeDtypeStruct((bsz, s, MIX_W), BF16)],
        scratch_shapes=[pltpu.VMEM((2, ML_H // 2, 128, 128), F32), pltpu.VMEM((2, 1, 256), F32),
                        pltpu.VMEM((2, 1, 128), F32)],
        compiler_params=_params(("arbitrary", "arbitrary")),
        name="mlstm_scan",
    )(p_ml, p_ml, p_ml, p_ml, p_rw, p_rw, p_rw, yf, yb, *consts)
    return (hf, hb), y_a


def _lru_dir(views, first, last, rev, d, cw_ref, cb_ref, wax_ref, bax_ref, lam_ref, carry):
    m_ref, p_ref, n_ref = views
    x = m_ref[0]
    pv = jnp.where(first, 0.0, p_ref[0, HALO - 1:HALO, :])
    nv = jnp.where(last, 0.0, n_ref[0, 0:2, :])
    cw = cw_ref[...]
    xc = (_shift_rows(x, pv, None, -1) * cw[0:1] + x * cw[1:2] + _shift_rows(x, None, nv, 1) * cw[2:3]
          + _shift_rows(x, None, nv, 2) * cw[3:4] + cb_ref[...])
    yield
    n = x.shape[0]
    ngroups = n // 8
    row8 = _iota((1, 8, 1), 1)
    sp = RG_C * _softplus(-lam_ref[d])
    a_parts, b_parts = [], []
    for i in range(n // CHUNK):
        xi = xc[CHUNK * i:CHUNK * (i + 1)]
        rr = _mm(xi, wax_ref[d]) + bax_ref[d]
        yield
        log_a = -sp * _sigmoid(rr[:, :MIX_W])
        av = jnp.exp(log_a)
        bv = jnp.sqrt(jnp.maximum(1.0 - jnp.exp(2.0 * log_a), 0.0)) * (_sigmoid(rr[:, MIX_W:]) * xi)
        yield
        av, bv = av.reshape(CHUNK // 8, 8, MIX_W), bv.reshape(CHUNK // 8, 8, MIX_W)
        for sh in (1, 2, 4):
            shift = (8 - sh) if rev else sh
            a_sh, b_sh = pltpu.roll(av, shift, 1), pltpu.roll(bv, shift, 1)
            valid = (row8 < 8 - sh) if rev else (row8 >= sh)
            bv = jnp.where(valid, av * b_sh + bv, bv)
            av = jnp.where(valid, av * a_sh, av)
        a_parts.append(av.reshape(CHUNK, MIX_W))
        b_parts.append(bv.reshape(CHUNK, MIX_W))
        yield
    av, bv = jnp.concatenate(a_parts, axis=0), jnp.concatenate(b_parts, axis=0)
    entering = [None] * ngroups
    h = carry
    for g in (range(ngroups - 1, -1, -1) if rev else range(ngroups)):
        r = 8 * g if rev else 8 * g + 7
        entering[g] = h
        h = bv[r:r + 1] + av[r:r + 1] * h
    h_in = jnp.concatenate([jnp.broadcast_to(e, (8, MIX_W)) for e in entering], axis=0)
    return bv + av * h_in, h


HG_LEVELS = (32, 16, 8, 4, 2, 1)


def _hgrn_lower_bound(lb_ref, layer, d):
    x = lb_ref[d]
    e = jnp.exp(x - jnp.max(x, axis=0, keepdims=True))
    sm = e / jnp.sum(e, axis=0, keepdims=True)
    acc = sm[0:1]
    for j in range(1, layer + 1):
        acc = acc + sm[j:j + 1]
    return jnp.clip(acc - sm[0:1], 0.0, 1.0)


def _hgrn_dir(q_ref, f_ref, i_ref, rev, d, layer, lb_ref, tri_ref, sel_ref, states):
    lb = _hgrn_lower_bound(lb_ref, layer, d)
    qraw, fp, v = q_ref[0], f_ref[0], i_ref[0]
    nchunk = qraw.shape[0] // HG_CHUNK
    q = qraw * _sigmoid(qraw)
    logf = jnp.log(lb + (1.0 - lb) * _sigmoid(fp))
    kd = (1.0 - lb) * _sigmoid(-fp)
    gc = _mm_exact_l(tri_ref[d], logf)
    rows = [slice(HG_CHUNK * i, HG_CHUNK * (i + 1)) for i in range(nchunk)]
    tot_rows = [gc[HG_CHUNK * i:HG_CHUNK * i + 1] if rev else gc[HG_CHUNK * (i + 1) - 1:HG_CHUNK * (i + 1)]
                for i in range(nchunk)]
    gtot = jnp.concatenate([jnp.broadcast_to(t, (HG_CHUNK, MIX_W)) for t in tot_rows], axis=0)
    q_in = q * jnp.exp(gc)
    k_out = kd * jnp.exp(gtot - gc)
    gcb = gc.astype(BF16)
    ref_chunks = [jnp.dot(sel_ref[d], gcb[rows[i]], preferred_element_type=F32) for i in range(nchunk)]
    refs = [jnp.concatenate([rc[HG_CHUNK * lv:HG_CHUNK * (lv + 1)] for rc in ref_chunks], axis=0)
            for lv in range(len(HG_LEVELS))]

    row = _iota((HG_CHUNK, HG_CHUNK), 0)
    col = _iota((HG_CHUNK, HG_CHUNK), 1)
    rowc = jnp.bitwise_and(_iota((q.shape[0], 1), 0), HG_CHUNK - 1)
    qs, ks, same = [], [], []
    for lv, half in enumerate(HG_LEVELS):
        upper = lambda idx: jnp.bitwise_and(idx, 2 * half - 1) >= half
        lower = lambda idx: jnp.bitwise_and(idx, 2 * half - 1) < half
        is_q = lower(rowc) if rev else upper(rowc)
        dlt = gc - refs[lv]
        x_l = jnp.where(is_q, q, kd) * jnp.exp(jnp.where(is_q, dlt, -dlt))
        qs.append(x_l)
        ks.append(x_l)
        sh = int(math.log2(2 * half))
        q_row = lower(row) if rev else upper(row)
        k_col = upper(col) if rev else lower(col)
        same.append(jnp.logical_and(jnp.right_shift(row, sh) == jnp.right_shift(col, sh),
                                    jnp.logical_and(q_row, k_col)))

    parts, att, upd = {}, {}, {}
    for i in range(nchunk):
        for hh in range(HG_H):
            sl = slice(128 * hh, 128 * (hh + 1))
            parts[i, hh] = [_mm_nt(ql[rows[i], sl], kl[rows[i], sl]) for ql, kl in zip(qs, ks)]
            upd[i, hh] = _mm_tn(v[rows[i], sl], k_out[rows[i], sl])
    qk = q * kd
    for (i, hh), plist in parts.items():
        diag = jnp.sum(qk[rows[i], 128 * hh:128 * (hh + 1)], axis=1, keepdims=True)
        acc = jnp.where(row == col, diag, 0.0)
        for part, msk in zip(plist, same):
            acc = acc + jnp.where(msk, part, 0.0)
        att[i, hh] = acc
    cur = list(states)
    before = {}
    for i in (range(nchunk - 1, -1, -1) if rev else range(nchunk)):
        g_dec = jnp.exp(tot_rows[i])
        for hh in range(HG_H):
            before[i, hh] = cur[hh]
            cur[hh] = cur[hh] * g_dec[:, 128 * hh:128 * (hh + 1)] + upd[i, hh]
    outs = [[None] * HG_H for _ in range(nchunk)]
    for i in range(nchunk):
        for hh in range(HG_H):
            sl = slice(128 * hh, 128 * (hh + 1))
            outs[i][hh] = _mm(att[i, hh], v[rows[i], sl]) + _mm_nt(q_in[rows[i], sl], before[i, hh])
    return outs, cur


def _hgrn_scan_kernel(layer, fq, ff, fi, bq, bf, bi, lb_ref, tri_ref, sel_ref, of_ref, ob_ref, s_ref):
    start = pl.program_id(1) == 0
    prev = [[jnp.where(start, 0.0, s_ref[d, hh]) for hh in range(HG_H)] for d in range(2)]
    res = [_hgrn_dir(fq, ff, fi, False, 0, layer, lb_ref, tri_ref, sel_ref, prev[0]),
           _hgrn_dir(bq, bf, bi, True, 1, layer, lb_ref, tri_ref, sel_ref, prev[1])]
    for d, o_ref in enumerate((of_ref, ob_ref)):
        outs, new_states = res[d]
        for i, heads in enumerate(outs):
            for hh, o in enumerate(heads):
                o_ref[0, HG_CHUNK * i:HG_CHUNK * (i + 1), 128 * hh:128 * (hh + 1)] = o
        for hh in range(HG_H):
            s_ref[d, hh] = new_states[hh]


def _hgrn_scan(p_hg, hg_lb, layer, prm):
    bsz, s, _ = p_hg.shape
    nc = s // SCAN_ROWS
    consts = (hg_lb, prm["trib"], prm["hg_sel"])
    fw = lambda cb: pl.BlockSpec((1, SCAN_ROWS, MIX_W), lambda b, c: (b, c, cb))
    bw = lambda cb: pl.BlockSpec((1, SCAN_ROWS, MIX_W), lambda b, c: (b, nc - 1 - c, cb))
    return pl.pallas_call(
        functools.partial(_hgrn_scan_kernel, layer),
        grid=(bsz, nc),
        in_specs=[fw(0), fw(1), fw(3), bw(0), bw(2), bw(3)] + [_full(x.shape) for x in consts],
        out_specs=[fw(0), bw(0)],
        out_shape=[jax.ShapeDtypeStruct((bsz, s, MIX_W), F32)] * 2,
        scratch_shapes=[pltpu.VMEM((2, HG_H, HG_D, HG_D), F32)],
        compiler_params=_params(("arbitrary", "arbitrary")),
        name="hgrn_scan",
    )(*([p_hg] * 6), *consts)


def _head_rms128(x, g):
    parts = []
    for hh in range(MIX_W // 128):
        xh = x[:, 128 * hh:128 * (hh + 1)]
        parts.append(xh * lax.rsqrt(jnp.mean(xh * xh, axis=-1, keepdims=True) + 1e-6))
    return jnp.concatenate(parts, axis=1) * g


def _gelu_tanh(x):
    cdf = 0.5 * (1.0 + jnp.tanh(math.sqrt(2.0 / math.pi) * (x + 0.044715 * (x * x * x))))
    return x * cdf


def _merge_kernel(u_ref, h_ref, ya_ref, mlf, mlb, mlo, lrf, lrb, lrg, hgf, hgb, hgg, gt_ref, mln_ref,
                  hgn_ref, wg_ref, wb_ref, wo_ref, o_ref):
    gates = _sigmoid(jnp.dot(u_ref[0], wg_ref[...], preferred_element_type=F32))
    y_b = _sigmoid(mlo[0]) * _head_rms128(mlf[0] + mlb[0], mln_ref[...])
    y_c = (lrf[0] + lrb[0]) * _gelu_tanh(lrg[0])
    gg = hgg[0]
    y_d = (gg * _sigmoid(gg)) * _head_rms128(hgf[0] + hgb[0], hgn_ref[...])
    merged = None
    for n, y in enumerate((ya_ref[0], y_b, y_c, y_d)):
        term = gates[:, D_MODEL * n:D_MODEL * (n + 1)] * _mm(y, wb_ref[n])
        merged = term if merged is None else merged + term
    o_ref[0] = h_ref[0] + gt_ref[0, 0] * _mm(merged, wo_ref[...])


def _merge(u, h, y_a, ml, p_ml, lru, p_lru, hg, p_hg, mod_l, prm):
    bsz, s, _ = h.shape
    row = lambda w, cb=0: pl.BlockSpec((1, MERGE_TILE, w), lambda b, i: (b, i, cb))
    consts = (prm["ml_norm"], prm["hg_norm"], prm["w_gate"], prm["w_branch"], prm["w_out"])
    return pl.pallas_call(
        _merge_kernel,
        grid=(bsz, s // MERGE_TILE),
        in_specs=[row(D_MODEL), row(D_MODEL), row(MIX_W),
                  row(MIX_W), row(MIX_W), row(MIX_W, 2),
                  row(MIX_W), row(MIX_W), row(MIX_W, 1),
                  row(MIX_W), row(MIX_W), row(MIX_W, 4),
                  _mod_spec(2)] + [_resident(x.shape) for x in consts],
        out_specs=row(D_MODEL),
        out_shape=jax.ShapeDtypeStruct((bsz, s, D_MODEL), F32),
        compiler_params=_params(("arbitrary", "arbitrary")),
        name="merge",
    )(u, h, y_a, ml[0], ml[1], p_ml, lru[0], lru[1], p_lru, hg[0], hg[1], p_hg, mod_l, *consts)


def _ffn_kernel(nt, final, hm, hp, hn, g_ref, sc_ref, sh_ref, gt_ref, wup_ref, cw_ref, cb_ref, wdn_ref,
                fg_ref, o_ref):
    i = pl.program_id(1)
    hx = jnp.concatenate([hp[0], hm[0], hn[0]], axis=0)
    n = hx.shape[0]
    row = _iota((n, 1), 0)
    valid = jnp.logical_and(jnp.logical_or(row >= HALO, i > 0),
                            jnp.logical_or(row < n - HALO, i < nt - 1))
    u2 = jnp.where(valid, _rmsnorm_rows(hx, g_ref[...]) * (1.0 + sc_ref[0, 0]) + sh_ref[0, 0], 0.0).astype(BF16)
    cw = cw_ref[...]
    cb = cb_ref[...]
    fc = D_FF // FF_SPLIT
    acc = None
    for j in range(FF_SPLIT):
        zs = []
        for base in (0, D_FF):
            sl = slice(base + fc * j, base + fc * (j + 1))
            y = jnp.dot(u2, wup_ref[:, sl], preferred_element_type=F32)
            z = (pltpu.roll(y, 1, 0) * cw[0:1, sl] + y * cw[1:2, sl] + pltpu.roll(y, n - 1, 0) * cw[2:3, sl]
                 + cb[:, sl])
            zs.append(z[HALO:n - HALO])
        val, gate = zs
        act = val * (gate * _sigmoid(gate))
        part = _mm(act, wdn_ref[fc * j:fc * (j + 1), :])
        acc = part if acc is None else acc + part
    out = hm[0] + gt_ref[0, 0] * acc
    if final:
        out = _rmsnorm_rows(out, fg_ref[...])
    o_ref[0] = out


def _conv_ffn(h, mod_l, prm, final_g, final):
    bsz, s, _ = h.shape
    nt = s // FFN_TILE
    consts = (prm["ffn_up"], prm["ffn_cw"], prm["ffn_cb"], prm["ffn_down"], final_g)
    row = pl.BlockSpec((1, FFN_TILE, D_MODEL), lambda b, i: (b, i, 0))
    return pl.pallas_call(
        functools.partial(_ffn_kernel, nt, final),
        grid=(bsz, nt),
        in_specs=_seq_views(D_MODEL, FFN_TILE, nt, 0, True)[:3]
        + [_full((1, D_MODEL)), _mod_spec(4), _mod_spec(3), _mod_spec(5)] + [_full(x.shape) for x in consts],
        out_specs=row,
        out_shape=jax.ShapeDtypeStruct((bsz, s, D_MODEL), F32),
        compiler_params=_params(("arbitrary", "arbitrary")),
        name="conv_ffn",
    )(h, h, h, prm["norm2_g"], mod_l, mod_l, mod_l, *consts)


def _block_diag(blocks):
    g, n, m = blocks.shape
    eye = jnp.eye(g, dtype=blocks.dtype)
    return (eye[:, None, :, None] * blocks[:, :, None, :]).reshape(g * n, g * m)


def _constants():
    idx = np.arange(CHUNK)
    tri = np.stack([idx[:, None] >= idx[None, :], idx[:, None] <= idx[None, :]]).astype(np.float32)
    hidx = np.arange(HG_CHUNK)
    sel = np.zeros((2, len(HG_LEVELS) * HG_CHUNK, HG_CHUNK), np.float32)
    for lv, half in enumerate(HG_LEVELS):
        start = (hidx // (2 * half)) * (2 * half)
        sel[0, HG_CHUNK * lv + hidx, start + half - 1] = 1.0
        sel[1, HG_CHUNK * lv + hidx, start + half] = 1.0
    lane = np.arange(MIX_W)
    seg = (lane[:, None] // RW_N == lane[None, :] // RW_N).astype(np.float32)
    ridx = np.arange(SCAN_ROWS)

    def block_tri(chunk):
        same = ridx[:, None] // chunk == ridx[None, :] // chunk
        return np.stack([same & (ridx[:, None] >= ridx[None, :]),
                         same & (ridx[:, None] <= ridx[None, :])]).astype(np.float32)

    trib, trib_big = block_tri(CHUNK), block_tri(BIG_CHUNK)
    lanes = np.stack([np.broadcast_to(np.arange(128) < 64, (CHUNK, 128)),
                      np.broadcast_to(np.arange(128) >= 64, (CHUNK, 128))]).astype(np.float32)
    return tuple(jnp.asarray(x, BF16) for x in (tri, sel, seg, trib, lanes, trib_big))


def _layer_params(l, consts, w_in, rw_mu, rw_w0, rw_w2, rw_a0, rw_a2, rw_g2, rw_kk, rw_ka, rw_rk,
                  rw_lnw, rw_lnb, ml_ibias, ml_fbias, ml_norm, lru_conv_w, lru_conv_b, lru_wa, lru_ba,
                  lru_wx, lru_bx, lru_lam, hg_norm, w_branch, w_out, norm2_g, ffn_up, ffn_conv_w,
                  ffn_conv_b, ffn_down):
    row = lambda x: x.reshape(1, -1)
    w = w_in[l]
    o_ml = RW_COLS
    o_lru = o_ml + 1552
    o_hg = o_lru + 2 * MIX_W
    o_gate = o_hg + 5 * MIX_W
    zeros = jnp.zeros((D_MODEL, 120), F32)
    w_ml = jnp.concatenate([w[:, o_ml:o_ml + 1536], w[:, o_ml + 1536:o_ml + 1544], zeros,
                            w[:, o_ml + 1544:o_ml + 1552], zeros], axis=1)
    z64 = jnp.zeros((64, MIX_W), F32)
    wlr = jnp.stack([jnp.concatenate([jnp.concatenate([rw_w2[l, d], z64], axis=1),
                                      jnp.concatenate([z64, rw_a2[l, d]], axis=1)], axis=0) for d in range(2)])
    pad = jnp.zeros((120,), F32)
    ml_bias = jnp.concatenate([ml_ibias[l].reshape(-1), pad, ml_fbias[l].reshape(-1), pad]).reshape(1, 256)
    wax = jnp.stack([jnp.concatenate([_block_diag(lru_wa[l, d]), _block_diag(lru_wx[l, d])], axis=1)
                     for d in range(2)])
    return {
        "tri": consts[0], "hg_sel": consts[1], "seg64": consts[2], "trib": consts[3], "lanes": consts[4],
        "trib_big": consts[5],
        "w_rw": w[:, :RW_COLS].astype(BF16), "w_ml": w_ml.astype(BF16),
        "w_lru": w[:, o_lru:o_hg].astype(BF16), "w_hg": w[:, o_hg:o_gate].astype(BF16),
        "w_gate": w[:, o_gate:].astype(BF16),
        "mu": row(rw_mu[l]),
        "w0a0": jnp.concatenate([rw_w0[l], rw_a0[l]], axis=1).reshape(2, 1, 2 * MIX_W),
        "wlr": wlr.astype(BF16), "kk": row(rw_kk[l]), "ka": row(rw_ka[l]), "rk": row(rw_rk[l]),
        "lnw": row(rw_lnw[l]), "lnb": row(rw_lnb[l]), "g2": rw_g2[l].astype(BF16),
        "ml_bias": ml_bias, "ml_norm": row(ml_norm[l]),
        "lru_cw": lru_conv_w[l], "lru_cb": row(lru_conv_b[l]), "lru_wax": wax.astype(BF16),
        "lru_bax": jnp.concatenate([lru_ba[l], lru_bx[l]], axis=1).reshape(2, 1, 2 * MIX_W),
        "lru_lam": lru_lam[l].reshape(2, 1, MIX_W),
        "hg_norm": row(hg_norm[l]),
        "w_branch": w_branch[l].astype(BF16), "w_out": w_out[l].astype(BF16),
        "norm2_g": row(norm2_g[l]), "ffn_up": ffn_up[l].astype(BF16), "ffn_cw": ffn_conv_w[l],
        "ffn_cb": row(ffn_conv_b[l]), "ffn_down": ffn_down[l].astype(BF16),
    }


def kernel(x, c, ada_w, ada_b, norm1_g, w_in, rw_mu, rw_w0, rw_w2, rw_a0, rw_a2, rw_g2, rw_kk, rw_ka, rw_rk, rw_lnw, rw_lnb, ml_ibias, ml_fbias, ml_norm, lru_conv_w, lru_conv_b, lru_wa, lru_ba, lru_wx, lru_bx, lru_lam, hg_lb, hg_norm, w_branch, w_out, norm2_g, ffn_up, ffn_conv_w, ffn_conv_b, ffn_down, final_g):
    depth = w_in.shape[0]
    assert x.shape[1] % MERGE_TILE == 0 and x.shape[2] == D_MODEL
    consts = _constants()
    mod = _ada_mod(c.astype(F32), ada_w, ada_b)
    h = x.astype(F32)
    final_row = final_g.reshape(1, D_MODEL)
    hg_lb = jnp.swapaxes(hg_lb.astype(F32), 0, 1)
    for l in range(depth):
        prm = _layer_params(l, consts, w_in, rw_mu, rw_w0, rw_w2, rw_a0, rw_a2, rw_g2, rw_kk, rw_ka,
                            rw_rk, rw_lnw, rw_lnb, ml_ibias, ml_fbias, ml_norm, lru_conv_w, lru_conv_b,
                            lru_wa, lru_ba, lru_wx, lru_bx, lru_lam, hg_norm, w_branch, w_out, norm2_g,
                            ffn_up, ffn_conv_w, ffn_conv_b, ffn_down)
        mod_l = mod[l]
        u, p_rw, p_ml, p_lru, p_hg = _project(h, mod_l, norm1_g[l].reshape(1, D_MODEL), prm["w_rw"],
                                              prm["w_ml"], prm["w_lru"], prm["w_hg"])
        rw, lru = _rwkv_lru_scan(p_rw, p_lru, prm)
        ml, y_a = _mlstm_scan(p_ml, p_rw, *rw, prm)
        hg = _hgrn_scan(p_hg, hg_lb, l, prm)
        h = _merge(u, h, y_a, ml, p_ml, lru, p_lru, hg, p_hg, mod_l, prm)
        h = _conv_ffn(h, mod_l, prm, final_row, l == depth - 1)
    return h
```
